```python
import math
import jax
import jax.numpy as jnp
from jax import lax
import numpy as np

D_MODEL = 1024
BATCH = 4
SEQ = 4096
DEPTH = 4
DEC_BATCH = 32
DEC_SEQ = 4
PAST_LEN = 8192
PAGE_SIZE = 128

H_GLA = 4
DK_GLA = 32
DV_GLA = 64
GLA_RANK = 16
GLA_TAU = 16.0
H_DIFF = 4
DK_DIFF = 64
DV_DIFF = 128
H_GDN = 4
DK_GDN = 64
DV_GDN = 64
CONV_W = 4
C_CONV = 2 * H_GDN * DK_GDN + H_GDN * DV_GDN
D_MIX = H_GLA * DV_GLA + H_DIFF * DV_DIFF + H_GDN * DV_GDN
IN_SIZES = (H_GLA * DK_GLA, H_GLA * DK_GLA, H_GLA * DV_GLA, GLA_RANK, H_GLA * DV_GLA,
            2 * H_DIFF * DK_DIFF, 2 * H_DIFF * DK_DIFF, H_DIFF * DV_DIFF,
            H_GDN * DK_GDN, H_GDN * DK_GDN, H_GDN * DV_GDN, H_GDN * DV_GDN, H_GDN, H_GDN)
D_IN = sum(IN_SIZES)
CHUNK = 64
Q_BLOCK = 128
N_MEM = 256
H_CA = 4
DH_CA = D_MODEL // H_CA
D_FF = -(-8 * D_MODEL // (3 * 256)) * 256
RMS_EPS = 1e-6
L2_EPS = 1e-6

kernel_name = 'hymba_style_gla_diff_gdn_decoder_step'


def rmsnorm(x, g):
    xf = x.astype(jnp.float32)
    y = xf * lax.rsqrt(jnp.mean(xf * xf, axis=-1, keepdims=True) + RMS_EPS)
    return (y * g.astype(jnp.float32)).astype(x.dtype)


def l2norm(x):
    xf = x.astype(jnp.float32)
    return (xf * lax.rsqrt(jnp.sum(xf * xf, axis=-1, keepdims=True) + L2_EPS)).astype(x.dtype)


def split_cols(x, sizes):
    outs, off = [], 0
    for s in sizes:
        outs.append(x[..., off:off + s])
        off += s
    return outs


def _pad_time(arrs, c):
    t = arrs[0].shape[1]
    pad = (-t) % c
    if pad == 0:
        return list(arrs)
    return [jnp.pad(a, ((0, 0), (0, pad)) + ((0, 0),) * (a.ndim - 2)) for a in arrs]


def _to_chunks(a, c):
    b, t, h, d = a.shape
    return a.reshape(b, t // c, c, h, d).transpose(1, 0, 3, 2, 4)


def _from_chunks(o, t):
    n, b, h, c, d = o.shape
    return o.transpose(1, 0, 3, 2, 4).reshape(b, n * c, h, d)[:, :t]


def gla_recurrence(q, k, v, log_a, s0):
    f32 = jnp.float32
    t = q.shape[1]
    c = min(CHUNK, t)
    q, k, v, log_a = _pad_time([a.astype(f32) for a in (q, k, v, log_a)], c)
    tri = jnp.tril(jnp.ones((c, c), dtype=bool))

    def step(s, inp):
        qc, kc, vc, gc = inp
        b = jnp.cumsum(gc, axis=2)
        o = jnp.einsum('bhck,bhkv->bhcv', qc * jnp.exp(b), s)
        rel = jnp.exp(jnp.where(tri[:, :, None], b[:, :, :, None, :] - b[:, :, None, :, :], -jnp.inf))
        att = jnp.einsum('bhik,bhjk,bhijk->bhij', qc, kc, rel)
        o = o + jnp.einsum('bhij,bhjv->bhiv', att, vc)
        b_end = b[:, :, -1:, :]
        s = s * jnp.exp(b_end[:, :, 0, :])[..., None] + jnp.einsum('bhck,bhcv->bhkv', kc * jnp.exp(b_end - b), vc)
        return s, o

    s, o = lax.scan(step, s0.astype(f32), tuple(_to_chunks(a, c) for a in (q, k, v, log_a)))
    return _from_chunks(o, t), s


def gdn_recurrence(q, k, v, g, beta, s0):
    f32 = jnp.float32
    t = q.shape[1]
    c = min(CHUNK, t)
    dv = v.shape[-1]
    q, k, v, g, beta = _pad_time([a.astype(f32) for a in (q, k, v, g[..., None], beta[..., None])], c)
    tri = jnp.tril(jnp.ones((c, c), dtype=bool))
    strict = jnp.tril(jnp.ones((c, c), dtype=f32), -1)
    eye = jnp.eye(c, dtype=f32)

    def step(s, inp):
        qc, kc, vc, gc, bc = inp
        gam = jnp.cumsum(gc[..., 0], axis=-1)
        bc = bc[..., 0]
        dec = jnp.exp(jnp.where(tri, gam[..., :, None] - gam[..., None, :], -jnp.inf))
        kb = kc * bc[..., None]
        m = jnp.einsum('bhik,bhjk->bhij', kb, kc) * dec * strict
        rhs = jnp.concatenate([vc * bc[..., None], kb * jnp.exp(gam)[..., None]], axis=-1)
        sol = lax.linalg.triangular_solve(m + eye, rhs, left_side=True, lower=True, unit_diagonal=True)
        value, kcum = sol[..., :dv], sol[..., dv:]
        v_new = value - jnp.einsum('bhck,bhkv->bhcv', kcum, s)
        att = jnp.einsum('bhik,bhjk->bhij', qc, kc) * dec
        o = jnp.einsum('bhck,bhkv->bhcv', qc * jnp.exp(gam)[..., None], s) + jnp.einsum('bhij,bhjv->bhiv', att, v_new)
        g_end = gam[..., -1:]
        s = s * jnp.exp(g_end)[..., None] + jnp.einsum('bhck,bhcv->bhkv', kc * jnp.exp(g_end - gam)[..., None], v_new)
        return s, o

    s, o = lax.scan(step, s0.astype(f32), tuple(_to_chunks(a, c) for a in (q, k, v, g, beta)))
    return _from_chunks(o, t), s


def causal_conv(x, buf, w):
    t = x.shape[1]
    xp = jnp.concatenate([buf.astype(x.dtype), x], axis=1)
    y = sum(xp[:, i:i + t] * w[i] for i in range(CONV_W))
    return jax.nn.silu(y), xp[:, xp.shape[1] - (CONV_W - 1):]


def diff_attention(q1, q2, k_all, v_all, q_pos, k_pos, slopes, lam):
    k1, k2 = k_all[..., :DK_DIFF], k_all[..., DK_DIFF:]
    dist = q_pos[:, None] - k_pos[None, :]
    bias = -slopes[:, None, None] * jnp.abs(dist).astype(jnp.float32)[None]
    causal = dist >= 0
    scale = DK_DIFF ** -0.5

    def amap(q, k):
        s = jnp.einsum('bqhd,bkhd->bhqk', q, k).astype(jnp.float32) * scale + bias
        return jax.nn.softmax(jnp.where(causal, s, -jnp.inf), axis=-1)

    p = amap(q1, k1) - lam * amap(q2, k2)
    return jnp.einsum('bhqk,bkhd->bqhd', p.astype(v_all.dtype), v_all)


def diff_attention_prompt(q1, q2, k_rows, v_rows, slopes, lam):
    b, t, h, _ = q1.shape
    nb = t // Q_BLOCK
    k_pos = jnp.arange(t, dtype=jnp.int32)

    def blk(a):
        return a.reshape(b, nb, Q_BLOCK, h, a.shape[-1]).transpose(1, 0, 2, 3, 4)

    def one(args):
        qb1, qb2, qpos = args
        return diff_attention(qb1, qb2, k_rows, v_rows, qpos, k_pos, slopes, lam)

    o = lax.map(one, (blk(q1), blk(q2), k_pos.reshape(nb, Q_BLOCK)))
    return o.transpose(1, 0, 2, 3, 4).reshape(b, t, h, -1)


def layer(x, mem_k, mem_v, past_k, past_v, s_gla, s_gdn, conv_buf, lam_init, slopes, p):
    f32 = jnp.float32
    b, t, _ = x.shape
    h = rmsnorm(x, p['norm_mix'])
    (gq, gk, gv, ga, gg, dq, dk, dv, nq, nk, nv, nz, na, nb) = split_cols(h @ p['w_in'], IN_SIZES)

    q = gq.reshape(b, t, H_GLA, DK_GLA) * (DK_GLA ** -0.5)
    k = gk.reshape(b, t, H_GLA, DK_GLA)
    v = gv.reshape(b, t, H_GLA, DV_GLA)
    log_a = jax.nn.log_sigmoid((ga @ p['gla_w_a2'] + p['gla_b_a']).astype(f32)) / GLA_TAU
    o_gla, s_gla_new = gla_recurrence(q, k, v, log_a.reshape(b, t, H_GLA, DK_GLA), s_gla)
    o_gla = rmsnorm(o_gla.astype(x.dtype), p['gla_onorm']) * jax.nn.silu(gg.reshape(b, t, H_GLA, DV_GLA))

    q12 = rmsnorm(dq.reshape(b, t, H_DIFF, 2, DK_DIFF), p['diff_qnorm'])
    k_rows = rmsnorm(dk.reshape(b, t, H_DIFF, 2, DK_DIFF), p['diff_knorm']).reshape(b, t, H_DIFF, 2 * DK_DIFF)
    v_rows = dv.reshape(b, t, H_DIFF, DV_DIFF)
    lam = (jnp.exp(jnp.sum(p['lam_q1'].astype(f32) * p['lam_k1'].astype(f32)))
           - jnp.exp(jnp.sum(p['lam_q2'].astype(f32) * p['lam_k2'].astype(f32))) + lam_init)
    q1, q2 = q12[..., 0, :], q12[..., 1, :]
    if past_k is None:
        o_diff = diff_attention_prompt(q1, q2, k_rows, v_rows, slopes, lam)
    else:
        t0 = past_k.shape[1]
        k_all = jnp.concatenate([past_k.astype(x.dtype), k_rows], axis=1)
        v_all = jnp.concatenate([past_v.astype(x.dtype), v_rows], axis=1)
        q_pos = t0 + jnp.arange(t, dtype=jnp.int32)
        k_pos = jnp.arange(t0 + t, dtype=jnp.int32)
        o_diff = diff_attention(q1, q2, k_all, v_all, q_pos, k_pos, slopes, lam)
    o_diff = rmsnorm(o_diff, p['diff_onorm']) * (1.0 - lam_init)

    qkv, conv_new = causal_conv(jnp.concatenate([nq, nk, nv], axis=-1), conv_buf, p['gdn_conv_w'])
    cq, ck, cv = split_cols(qkv, (H_GDN * DK_GDN, H_GDN * DK_GDN, H_GDN * DV_GDN))
    cq = l2norm(cq.reshape(b, t, H_GDN, DK_GDN)) * (DK_GDN ** -0.5)
    ck = l2norm(ck.reshape(b, t, H_GDN, DK_GDN))
    cv = cv.reshape(b, t, H_GDN, DV_GDN)
    beta = jax.nn.sigmoid(nb.astype(f32))
    g = -jnp.exp(p['gdn_a_log'].astype(f32)) * jax.nn.softplus(na.astype(f32) + p['gdn_dt_bias'].astype(f32))
    o_gdn, s_gdn_new = gdn_recurrence(cq, ck, cv, g, beta, s_gdn)
    o_gdn = rmsnorm(o_gdn.astype(x.dtype), p['gdn_onorm']) * jax.nn.silu(nz.reshape(b, t, H_GDN, DV_GDN))

    mix = jnp.concatenate([o_gla.reshape(b, t, -1), o_diff.reshape(b, t, -1), o_gdn.reshape(b, t, -1)], axis=-1)
    x = x + mix @ p['w_out']

    hc = rmsnorm(x, p['norm_ca'])
    qc = rmsnorm((hc @ p['ca_wq']).reshape(b, t, H_CA, DH_CA), p['ca_qnorm'])
    s = jnp.einsum('bqhd,bmhd->bhqm', qc, mem_k.astype(x.dtype)).astype(f32) * (DH_CA ** -0.5)
    a = jax.nn.softmax(s, axis=-1).astype(x.dtype)
    oc = jnp.einsum('bhqm,bmhd->bqhd', a, mem_v.astype(x.dtype)).reshape(b, t, H_CA * DH_CA)
    x = x + oc @ p['ca_wo']

    hf = rmsnorm(x, p['norm_ffn'])
    x = x + (jax.nn.silu(hf @ p['ffn_w1']) * (hf @ p['ffn_w3'])) @ p['ffn_w2']
    return x, k_rows, v_rows, s_gla_new.astype(x.dtype), s_gdn_new.astype(x.dtype), conv_new


def setup_inputs(seed: int = 0) -> dict:
    key = jax.random.key(seed)
    ks = iter(jax.random.split(key, 64))
    f32 = jnp.float32

    def nrm(shape, scale=1.0):
        return jax.random.normal(next(ks), shape, f32) * scale

    def gain(shape):
        return 1.0 + 0.01 * nrm(shape)

    n_pages = PAST_LEN // PAGE_SIZE
    n_pool = (DEC_BATCH * n_pages * 5) // 4
    page_table = jax.random.permutation(next(ks), n_pool)[:DEC_BATCH * n_pages].reshape(DEC_BATCH, n_pages).astype(jnp.int32)
    dt = jnp.exp(jax.random.uniform(next(ks), (DEPTH, H_GDN), f32, math.log(1e-3), math.log(1e-1)))
    dt_bias = dt + jnp.log(-jnp.expm1(-dt))
    a_log = jnp.log(jax.random.uniform(next(ks), (DEPTH, H_GDN), f32, 1.0, 16.0))
    return {
        'x_prompt': nrm((BATCH, SEQ, D_MODEL)),
        'x_sample': nrm((DEC_BATCH, DEC_SEQ, D_MODEL)),
        'mem_prompt': nrm((BATCH, N_MEM, D_MODEL)),
        'cache_k': nrm((DEPTH, n_pool, PAGE_SIZE, H_DIFF, 2 * DK_DIFF)),
        'cache_v': nrm((DEPTH, n_pool, PAGE_SIZE, H_DIFF, DV_DIFF)),
        'cache_mem_k': nrm((DEPTH, DEC_BATCH, N_MEM, H_CA, DH_CA)),
        'cache_mem_v': nrm((DEPTH, DEC_BATCH, N_MEM, H_CA, DH_CA)),
        'state_gla': nrm((DEPTH, DEC_BATCH, H_GLA, DK_GLA, DV_GLA), 0.3),
        'state_gdn': nrm((DEPTH, DEC_BATCH, H_GDN, DK_GDN, DV_GDN), 0.1),
        'state_conv': nrm((DEPTH, DEC_BATCH, CONV_W - 1, C_CONV)),
        'page_table': page_table,
        'norm_mix': gain((DEPTH, D_MODEL)),
        'w_in': nrm((DEPTH, D_MODEL, D_IN), D_MODEL ** -0.5),
        'gla_w_a2': nrm((DEPTH, GLA_RANK, H_GLA * DK_GLA), GLA_RANK ** -0.5),
        'gla_b_a': nrm((DEPTH, H_GLA * DK_GLA), 0.1),
        'gla_onorm': gain((DEPTH, DV_GLA)),
        'diff_qnorm': gain((DEPTH, DK_DIFF)),
        'diff_knorm': gain((DEPTH, DK_DIFF)),
        'lam_q1': nrm((DEPTH, DK_DIFF), 0.1),
        'lam_k1': nrm((DEPTH, DK_DIFF), 0.1),
        'lam_q2': nrm((DEPTH, DK_DIFF), 0.1),
        'lam_k2': nrm((DEPTH, DK_DIFF), 0.1),
        'diff_onorm': gain((DEPTH, DV_DIFF)),
        'gdn_conv_w': nrm((DEPTH, CONV_W, C_CONV), CONV_W ** -0.5),
        'gdn_a_log': a_log,
        'gdn_dt_bias': dt_bias,
        'gdn_onorm': gain((DEPTH, DV_GDN)),
        'w_out': nrm((DEPTH, D_MIX, D_MODEL), D_MIX ** -0.5),
        'norm_ca': gain((DEPTH, D_MODEL)),
        'norm_mem': gain((DEPTH, D_MODEL)),
        'ca_wq': nrm((DEPTH, D_MODEL, H_CA * DH_CA), D_MODEL ** -0.5),
        'ca_wk': nrm((DEPTH, D_MODEL, H_CA * DH_CA), D_MODEL ** -0.5),
        'ca_wv': nrm((DEPTH, D_MODEL, H_CA * DH_CA), D_MODEL ** -0.5),
        'ca_wo': nrm((DEPTH, H_CA * DH_CA, D_MODEL), (H_CA * DH_CA) ** -0.5),
        'ca_qnorm': gain((DEPTH, DH_CA)),
        'ca_knorm': gain((DEPTH, DH_CA)),
        'norm_ffn': gain((DEPTH, D_MODEL)),
        'ffn_w1': nrm((DEPTH, D_MODEL, D_FF), D_MODEL ** -0.5),
        'ffn_w3': nrm((DEPTH, D_MODEL, D_FF), D_MODEL ** -0.5),
        'ffn_w2': nrm((DEPTH, D_FF, D_MODEL), D_FF ** -0.5),
    }


def reference(x_prompt, x_sample, mem_prompt, cache_k, cache_v, cache_mem_k, cache_mem_v, state_gla, state_gdn,
              state_conv, page_table, norm_mix, w_in, gla_w_a2, gla_b_a, gla_onorm, diff_qnorm, diff_knorm,
              lam_q1, lam_k1, lam_q2, lam_k2, diff_onorm, gdn_conv_w, gdn_a_log, gdn_dt_bias, gdn_onorm, w_out,
              norm_ca, norm_mem, ca_wq, ca_wk, ca_wv, ca_wo, ca_qnorm, ca_knorm, norm_ffn, ffn_w1, ffn_w3, ffn_w2):
    slopes = 2.0 ** (-8.0 * jnp.arange(1, H_DIFF + 1, dtype=jnp.float32) / H_DIFF)
    xp, xs = x_prompt, x_sample
    bp, bs = xp.shape[0], xs.shape[0]
    kp_l, vp_l, ks_l, vs_l = [], [], [], []
    glap_l, glas_l, gdnp_l, gdns_l, convp_l, convs_l, memk_l, memv_l = [], [], [], [], [], [], [], []
    for l in range(DEPTH):
        p = {'norm_mix': norm_mix[l], 'w_in': w_in[l], 'gla_w_a2': gla_w_a2[l], 'gla_b_a': gla_b_a[l],
             'gla_onorm': gla_onorm[l], 'diff_qnorm': diff_qnorm[l], 'diff_knorm': diff_knorm[l],
             'lam_q1': lam_q1[l], 'lam_k1': lam_k1[l], 'lam_q2': lam_q2[l], 'lam_k2': lam_k2[l],
             'diff_onorm': diff_onorm[l], 'gdn_conv_w': gdn_conv_w[l], 'gdn_a_log': gdn_a_log[l],
             'gdn_dt_bias': gdn_dt_bias[l], 'gdn_onorm': gdn_onorm[l], 'w_out': w_out[l],
             'norm_ca': norm_ca[l], 'ca_wq': ca_wq[l], 'ca_wo': ca_wo[l], 'ca_qnorm': ca_qnorm[l],
             'norm_ffn': norm_ffn[l], 'ffn_w1': ffn_w1[l], 'ffn_w3': ffn_w3[l], 'ffn_w2': ffn_w2[l]}
        lam_init = 0.8 - 0.6 * math.exp(-0.3 * l)

        hm = rmsnorm(mem_prompt, norm_mem[l])
        mk = rmsnorm((hm @ ca_wk[l]).reshape(bp, -1, H_CA, DH_CA), ca_knorm[l])
        mv = (hm @ ca_wv[l]).reshape(bp, -1, H_CA, DH_CA)
        z_gla = jnp.zeros((bp, H_GLA, DK_GLA, DV_GLA), xp.dtype)
        z_gdn = jnp.zeros((bp, H_GDN, DK_GDN, DV_GDN), xp.dtype)
        z_conv = jnp.zeros((bp, CONV_W - 1, C_CONV), xp.dtype)
        xp, kp, vp, sgp, sdp, cvp = layer(xp, mk, mv, None, None, z_gla, z_gdn, z_conv, lam_init, slopes, p)

        past_k = cache_k[l][page_table].reshape(bs, -1, H_DIFF, 2 * DK_DIFF)
        past_v = cache_v[l][page_table].reshape(bs, -1, H_DIFF, DV_DIFF)
        xs, ks, vs, sgs, sds, cvs = layer(xs, cache_mem_k[l], cache_mem_v[l], past_k, past_v,
                                          state_gla[l], state_gdn[l], state_conv[l], lam_init, slopes, p)

        kp_l.append(kp); vp_l.append(vp); ks_l.append(ks); vs_l.append(vs)
        glap_l.append(sgp); glas_l.append(sgs); gdnp_l.append(sdp); gdns_l.append(sds)
        convp_l.append(cvp); convs_l.append(cvs); memk_l.append(mk); memv_l.append(mv)

    return (xp, xs, jnp.stack(kp_l), jnp.stack(vp_l), jnp.stack(ks_l), jnp.stack(vs_l),
            jnp.stack(glap_l), jnp.stack(glas_l), jnp.stack(gdnp_l), jnp.stack(gdns_l),
            jnp.stack(convp_l), jnp.stack(convs_l), jnp.stack(memk_l), jnp.stack(memv_l))
```

```python
import functools
import math

import jax
import jax.numpy as jnp
from jax import lax
from jax.experimental import pallas as pl
from jax.experimental.pallas import tpu as pltpu

F32 = jnp.float32
BF16 = jnp.bfloat16

D_MODEL = 1024
DEPTH = 4
PAGE_SIZE = 128
H_GLA, DK_GLA, DV_GLA, GLA_RANK, GLA_TAU = 4, 32, 64, 16, 16.0
H_DIFF, DK_DIFF, DV_DIFF = 4, 64, 128
H_GDN, DK_GDN, DV_GDN, CONV_W = 4, 64, 64, 4
C_CONV = 2 * H_GDN * DK_GDN + H_GDN * DV_GDN
N_MEM, H_CA = 256, 4
DH_CA = D_MODEL // H_CA
D_FF = -(-8 * D_MODEL // (3 * 256)) * 256
RMS_EPS = 1e-6
L2_EPS = 1e-6

VMEM_LIMIT_V7X = 56 * 1024 * 1024
LANES = 128

GLA_W = 896
GDN_W = 1152
DIFF_W = 512
OFF_GLA, OFF_DQ, OFF_DK, OFF_DV, OFF_GDN = 0, 896, 1408, 1920, 2432
IN_W = OFF_GDN + GDN_W

NEG_BIG = -1e30


def _dot(a, b):
    return jnp.dot(a, b, preferred_element_type=F32)


def _dot_nt(a, b):
    return lax.dot_general(a, b, (((1,), (1,)), ((), ())), preferred_element_type=F32)


def _hilo(a):
    hi = a.astype(BF16)
    lo = (a - hi.astype(F32)).astype(BF16)
    return hi, lo


def _dot_sel_rhs(a, sel):
    hi, lo = _hilo(a)
    return _dot(hi, sel) + _dot(lo, sel)


def _dot_sel_lhs(sel, b):
    hi, lo = _hilo(b)
    return _dot(sel, hi) + _dot(sel, lo)


def _dot3(a, b):
    ah, al = _hilo(a)
    bh, bl = _hilo(b)
    return _dot(ah, bh) + _dot(ah, bl) + _dot(al, bh)


def _sigmoid(x):
    return 1.0 / (1.0 + jnp.exp(-x))


def _softplus(x):
    return jnp.maximum(x, 0.0) + jnp.log(1.0 + jnp.exp(-jnp.abs(x)))


def _rms_rows(x, g):
    ms = jnp.mean(x * x, axis=-1, keepdims=True)
    return x * lax.rsqrt(ms + RMS_EPS) * g


def _pad_rows_128(a):
    r = a.shape[0]
    if r == LANES:
        return a
    return jnp.concatenate([a, jnp.zeros((LANES - r, a.shape[1]), a.dtype)], axis=0)


def _const_spec(shape):
    nd = len(shape)
    return pl.BlockSpec(shape, lambda *_: (0,) * nd)


def _params(sem):
    return pltpu.CompilerParams(dimension_semantics=sem, vmem_limit_bytes=VMEM_LIMIT_V7X)


def _in_proj_kernel(x_ref, g_ref, w_ref, qkg_ref, p64_ref,
                    gla_ref, q_ref, k_ref, kb_ref, v_ref, vb_ref, gdn_ref):
    xn = _rms_rows(x_ref[...], g_ref[...]).astype(BF16)
    gla_ref[...] = _dot(xn, w_ref[:, OFF_GLA:OFF_GLA + GLA_W])

    def qknorm(y, gain):
        ss = _dot_sel_rhs(y * y, p64_ref[...])
        return y * lax.rsqrt(ss * (1.0 / DK_DIFF) + RMS_EPS) * gain

    yq = _dot(xn, w_ref[:, OFF_DQ:OFF_DQ + DIFF_W])
    q_ref[...] = qknorm(yq, qkg_ref[:, 0:DIFF_W]).astype(BF16)
    yk = _dot(xn, w_ref[:, OFF_DK:OFF_DK + DIFF_W])
    kk = qknorm(yk, qkg_ref[:, DIFF_W:2 * DIFF_W])
    k_ref[...] = kk
    kb_ref[...] = kk.astype(BF16)
    yv = _dot(xn, w_ref[:, OFF_DV:OFF_DV + DIFF_W])
    v_ref[...] = yv
    vb_ref[...] = yv.astype(BF16)
    gdn_ref[...] = _dot(xn, w_ref[:, OFF_GDN:OFF_GDN + GDN_W])


def _in_proj(x, g, w, qkg, p64, tm):
    m = x.shape[0]
    row = lambda wd: pl.BlockSpec((tm, wd), lambda i: (i, 0))
    return pl.pallas_call(
        _in_proj_kernel,
        grid=(m // tm,),
        in_specs=[row(D_MODEL), _const_spec((1, D_MODEL)), _const_spec((D_MODEL, IN_W)),
                  _const_spec((1, 2 * DIFF_W)), _const_spec((DIFF_W, DIFF_W))],
        out_specs=[row(GLA_W), row(DIFF_W), row(DIFF_W), row(DIFF_W), row(DIFF_W), row(DIFF_W), row(GDN_W)],
        out_shape=[jax.ShapeDtypeStruct((m, GLA_W), F32), jax.ShapeDtypeStruct((m, DIFF_W), BF16),
                   jax.ShapeDtypeStruct((m, DIFF_W), F32), jax.ShapeDtypeStruct((m, DIFF_W), BF16),
                   jax.ShapeDtypeStruct((m, DIFF_W), F32), jax.ShapeDtypeStruct((m, DIFF_W), BF16),
                   jax.ShapeDtypeStruct((m, GDN_W), F32)],
        compiler_params=_params(("parallel",)),
        name="in_proj",
    )(x, g, w, qkg, p64)


def _norm_mm_kernel(x_ref, g_ref, w_ref, hg_ref, o_ref, *, headnorm, scale):
    xn = _rms_rows(x_ref[...], g_ref[...]).astype(BF16)
    y = _dot(xn, w_ref[...])
    if headnorm:
        for h in range(H_CA):
            yh = y[:, h * DH_CA:(h + 1) * DH_CA]
            o_ref[:, h * DH_CA:(h + 1) * DH_CA] = (_rms_rows(yh, hg_ref[...]) * scale).astype(o_ref.dtype)
    else:
        o_ref[...] = y.astype(o_ref.dtype)


def _norm_mm(x, g, w, hg, tm, *, headnorm, scale, out_dtype):
    m, n = x.shape[0], w.shape[1]
    return pl.pallas_call(
        functools.partial(_norm_mm_kernel, headnorm=headnorm, scale=scale),
        grid=(m // tm,),
        in_specs=[pl.BlockSpec((tm, D_MODEL), lambda i: (i, 0)), _const_spec((1, D_MODEL)),
                  _const_spec((D_MODEL, n)), _const_spec((1, DH_CA))],
        out_specs=pl.BlockSpec((tm, n), lambda i: (i, 0)),
        out_shape=jax.ShapeDtypeStruct((m, n), out_dtype),
        compiler_params=_params(("parallel",)),
        name="norm_mm",
    )(x, g, w, hg)


def _mm_res_kernel(*refs, n):
    xs, ws, res_ref, o_ref = refs[:n], refs[n:2 * n], refs[2 * n], refs[2 * n + 1]
    acc = res_ref[...]
    for x_ref, w_ref in zip(xs, ws):
        acc = acc + _dot(x_ref[...], w_ref[...])
    o_ref[...] = acc


def _mm_res(xs, ws, res, tm):
    m, n = res.shape
    k = len(xs)
    in_specs = ([pl.BlockSpec((tm, x.shape[1]), lambda i: (i, 0)) for x in xs]
                + [_const_spec(w.shape) for w in ws]
                + [pl.BlockSpec((tm, n), lambda i: (i, 0))])
    return pl.pallas_call(
        functools.partial(_mm_res_kernel, n=k),
        grid=(m // tm,),
        in_specs=in_specs,
        out_specs=pl.BlockSpec((tm, n), lambda i: (i, 0)),
        out_shape=jax.ShapeDtypeStruct((m, n), F32),
        compiler_params=_params(("parallel",)),
        name="mm_res",
    )(*xs, *ws, res)


FF_CHUNK = D_FF // 2


def _swiglu_kernel(x_ref, g_ref, w1_ref, w3_ref, o_ref):
    xn = _rms_rows(x_ref[...], g_ref[...]).astype(BF16)
    for c in range(D_FF // FF_CHUNK):
        sl = slice(c * FF_CHUNK, (c + 1) * FF_CHUNK)
        a = _dot(xn, w1_ref[:, sl])
        b = _dot(xn, w3_ref[:, sl])
        o_ref[:, sl] = (a * _sigmoid(a) * b).astype(BF16)


def _swiglu(x, g, w1, w3, tm):
    m = x.shape[0]
    return pl.pallas_call(
        _swiglu_kernel,
        grid=(m // tm,),
        in_specs=[pl.BlockSpec((tm, D_MODEL), lambda i: (i, 0)), _const_spec((1, D_MODEL)),
                  _const_spec((D_MODEL, D_FF)), _const_spec((D_MODEL, D_FF))],
        out_specs=pl.BlockSpec((tm, D_FF), lambda i: (i, 0)),
        out_shape=jax.ShapeDtypeStruct((m, D_FF), BF16),
        compiler_params=_params(("parallel",)),
        name="swiglu",
    )(x, g, w1, w3)


def _gla_kernel(p_ref, s0_ref, wa_ref, ba_ref, on_ref, tri_ref, seg_ref, p64_ref, bd_ref,
                o_ref, sout_ref, s_s, b_s, k_s, v_s, *, chunk, n_valid):
    c = pl.program_id(1)

    @pl.when(c == 0)
    def _():
        s_s[...] = s0_ref[0]

    p = p_ref[...]
    q = p[:, 0:128] * (DK_GLA ** -0.5)
    k = p[:, 128:256]
    v = p[:, 256:512]
    gate = p[:, 512:768]
    x = _dot(p[:, 768:896].astype(BF16), wa_ref[...]) + ba_ref[...]
    la = (jnp.minimum(x, 0.0) - jnp.log(1.0 + jnp.exp(-jnp.abs(x)))) * (1.0 / GLA_TAU)
    row = lax.broadcasted_iota(jnp.int32, (chunk, 1), 0)
    if n_valid < chunk:
        la = jnp.where(row < n_valid, la, 0.0)
        k = jnp.where(row < n_valid, k, 0.0)
    b = _dot_sel_lhs(tri_ref[...], la)
    b_s[...] = b
    k_s[...] = k
    v_s[...] = v
    s = s_s[...]
    o = _dot((q * jnp.exp(b)).astype(BF16), s.astype(BF16))

    def intra(j, o):
        bj = b_s[pl.ds(j, 1), :]
        kj = k_s[pl.ds(j, 1), :]
        vj = v_s[pl.ds(j, 1), :]
        e = jnp.exp(jnp.where(row >= j, b - bj, -jnp.inf))
        term = (q * kj * e).astype(BF16)
        return o + _dot(term, seg_ref[...]) * vj

    o = lax.fori_loop(0, chunk, intra, o)

    b_end = b[chunk - 1:chunk, :]
    kd = k * jnp.exp(b_end - b)
    kd_t = jnp.transpose(_pad_rows_128(kd)).astype(BF16)
    upd = _dot(kd_t, _pad_rows_128(v).astype(BF16))
    dcol = jnp.transpose(jnp.broadcast_to(jnp.exp(b_end), (LANES, LANES)))[:, 0:1]
    s_new = s * dcol + upd * bd_ref[...]
    s_s[...] = s_new

    ss = _dot_sel_rhs(o * o, p64_ref[...])
    on = o * lax.rsqrt(ss * (1.0 / DV_GLA) + RMS_EPS) * on_ref[...]
    o_ref[...] = (on * gate * _sigmoid(gate)).astype(BF16)

    @pl.when(c == pl.num_programs(1) - 1)
    def _():
        sout_ref[0] = s_new


def _gla(proj, s0, wa, ba, onorm, consts, bc, t, chunk, n_valid):
    nc = t // chunk
    hk, hv = H_GLA * DK_GLA, H_GLA * DV_GLA
    return pl.pallas_call(
        functools.partial(_gla_kernel, chunk=chunk, n_valid=n_valid),
        grid=(bc, nc),
        in_specs=[pl.BlockSpec((chunk, GLA_W), lambda b, c: (b * nc + c, 0)),
                  pl.BlockSpec((1, hk, hv), lambda b, c: (b, 0, 0)),
                  _const_spec((LANES, hk)), _const_spec((1, hk)), _const_spec((1, hv)),
                  _const_spec((chunk, chunk)), _const_spec((hk, hv)), _const_spec((hv, hv)),
                  _const_spec((hk, hv))],
        out_specs=[pl.BlockSpec((chunk, hv), lambda b, c: (b * nc + c, 0)),
                   pl.BlockSpec((1, hk, hv), lambda b, c: (b, 0, 0))],
        out_shape=[jax.ShapeDtypeStruct((bc * t, hv), BF16), jax.ShapeDtypeStruct((bc, hk, hv), F32)],
        scratch_shapes=[pltpu.VMEM((hk, hv), F32), pltpu.VMEM((chunk, hk), F32),
                        pltpu.VMEM((chunk, hk), F32), pltpu.VMEM((chunk, hv), F32)],
        compiler_params=_params(("parallel", "arbitrary")),
        name="gla",
    )(proj, s0, wa, ba, onorm, consts["tri"], consts["seg_gla"], consts["p64_256"], consts["bd_gla"])


def _gdn_kernel(p_ref, s0_ref, cb_ref, cw_ref, alog_ref, dtb_ref, on_ref,
                tri_ref, p64_ref, eg_ref, eg64_ref, eb64_ref, bd_ref,
                o_ref, sout_ref, cout_ref, s_s, xp_s, *, chunk, n_valid):
    c = pl.program_id(1)
    hd = H_GDN * DK_GDN

    @pl.when(c == 0)
    def _():
        s_s[...] = s0_ref[0]
        xp_s[5:8, :] = cb_ref[0]

    p = p_ref[...]
    xp_s[8:8 + chunk, :] = p[:, 0:C_CONV]
    y = xp_s[pl.ds(5, chunk), :] * cw_ref[0:1, :]
    for i in range(1, CONV_W):
        y = y + xp_s[pl.ds(5 + i, chunk), :] * cw_ref[i:i + 1, :]
    y = y * _sigmoid(y)
    tail = xp_s[pl.ds(5 + n_valid, CONV_W - 1), :]
    xp_s[5:8, :] = tail

    @pl.when(c == pl.num_programs(1) - 1)
    def _():
        cout_ref[0] = tail

    def l2n(a):
        ss = _dot_sel_rhs(a * a, p64_ref[...])
        return a * lax.rsqrt(ss + L2_EPS)

    cq = l2n(y[:, 0:hd]) * (DK_GDN ** -0.5)
    ck = l2n(y[:, hd:2 * hd])
    cv = y[:, 2 * hd:3 * hd]
    z = p[:, 768:1024]
    small = p[:, 1024:1152]
    g = -jnp.exp(alog_ref[...]) * _softplus(small + dtb_ref[...])
    beta = _sigmoid(small)
    if n_valid < chunk:
        rowv = lax.broadcasted_iota(jnp.int32, (chunk, 1), 0) < n_valid
        g = jnp.where(rowv, g, 0.0)
        beta = jnp.where(rowv, beta, 0.0)
    gam = _dot_sel_lhs(tri_ref[...], g)
    gam_w = _dot_sel_rhs(gam, eg_ref[...])
    gam_b = _dot_sel_rhs(gam, eg64_ref[...])
    beta_b = _dot_sel_rhs(beta, eb64_ref[...])
    gam_t = jnp.transpose(_pad_rows_128(gam))

    eg = jnp.exp(gam_b)
    kb = ck * beta_b
    kbg = kb * eg
    vb = cv * beta_b
    g_end = gam_b[chunk - 1:chunk, :]
    kdec = ck * jnp.exp(g_end - gam_b)
    s = s_s[...]
    s16 = s.astype(BF16)
    ck16 = ck.astype(BF16)

    ri = lax.broadcasted_iota(jnp.int32, (chunk, chunk), 0)
    ci = lax.broadcasted_iota(jnp.int32, (chunk, chunk), 1)
    eye = (ri == ci).astype(F32)
    lane = lax.broadcasted_iota(jnp.int32, (1, hd), 1)
    n_sq = int(math.log2(chunk)) - 1

    value = jnp.zeros((chunk, hd), F32)
    kcum = jnp.zeros((chunk, hd), F32)
    atts = []
    for h in range(H_GDN):
        lm = (lane >= h * DK_GDN) & (lane < (h + 1) * DK_GDN)
        gcol = gam_w[:, h * LANES:h * LANES + chunk]
        grow = gam_t[h:h + 1, 0:chunk]
        dec = jnp.exp(jnp.where(ri >= ci, gcol - grow, -jnp.inf))
        m = _dot_nt(jnp.where(lm, kb, 0.0).astype(BF16), ck16) * jnp.where(ri > ci, dec, 0.0)
        xk = -m
        tinv = eye + xk
        pw = xk
        for _ in range(n_sq):
            pw = _dot3(pw, pw)
            tinv = tinv + _dot3(tinv, pw)
        rhs = jnp.concatenate([jnp.where(lm, vb, 0.0), jnp.where(lm, kbg, 0.0)], axis=1)
        sol = _dot3(tinv, rhs)
        value = value + sol[:, 0:hd]
        kcum = kcum + sol[:, hd:2 * hd]
        atts.append((_dot_nt(jnp.where(lm, cq, 0.0).astype(BF16), ck16) * dec).astype(BF16))

    v_new = value - _dot(kcum.astype(BF16), s16)
    o = _dot((cq * eg).astype(BF16), s16)
    for h in range(H_GDN):
        lm = (lane >= h * DK_GDN) & (lane < (h + 1) * DK_GDN)
        o = o + _dot(atts[h], jnp.where(lm, v_new, 0.0).astype(BF16))
    kdec_t = jnp.transpose(_pad_rows_128(kdec)).astype(BF16)
    upd = _dot(kdec_t, _pad_rows_128(v_new).astype(BF16))
    s_new = s * jnp.exp(g_end) + upd * bd_ref[...]
    s_s[...] = s_new

    ss = _dot_sel_rhs(o * o, p64_ref[...])
    on = o * lax.rsqrt(ss * (1.0 / DV_GDN) + RMS_EPS) * on_ref[...]
    o_ref[...] = (on * z * _sigmoid(z)).astype(BF16)

    @pl.when(c == pl.num_programs(1) - 1)
    def _():
        sout_ref[0] = s_new


def _gdn(proj, s0, conv0, cw, alog, dtb, onorm, consts, bc, t, chunk, n_valid):
    nc = t // chunk
    hd = H_GDN * DK_GDN
    return pl.pallas_call(
        functools.partial(_gdn_kernel, chunk=chunk, n_valid=n_valid),
        grid=(bc, nc),
        in_specs=[pl.BlockSpec((chunk, GDN_W), lambda b, c: (b * nc + c, 0)),
                  pl.BlockSpec((1, hd, hd), lambda b, c: (b, 0, 0)),
                  pl.BlockSpec((1, CONV_W - 1, C_CONV), lambda b, c: (b, 0, 0)),
                  _const_spec((CONV_W, C_CONV)), _const_spec((1, LANES)), _const_spec((1, LANES)),
                  _const_spec((1, hd)),
                  _const_spec((chunk, chunk)), _const_spec((hd, hd)), _const_spec((LANES, 4 * LANES)),
                  _const_spec((LANES, hd)), _const_spec((LANES, hd)), _const_spec((hd, hd))],
        out_specs=[pl.BlockSpec((chunk, hd), lambda b, c: (b * nc + c, 0)),
                   pl.BlockSpec((1, hd, hd), lambda b, c: (b, 0, 0)),
                   pl.BlockSpec((1, CONV_W - 1, C_CONV), lambda b, c: (b, 0, 0))],
        out_shape=[jax.ShapeDtypeStruct((bc * t, hd), BF16), jax.ShapeDtypeStruct((bc, hd, hd), F32),
                   jax.ShapeDtypeStruct((bc, CONV_W - 1, C_CONV), F32)],
        scratch_shapes=[pltpu.VMEM((hd, hd), F32), pltpu.VMEM((chunk + 8, C_CONV), F32)],
        compiler_params=_params(("parallel", "arbitrary")),
        name="gdn",
    )(proj, s0, conv0, cw, alog, dtb, onorm, consts["tri"], consts["p64_256"], consts["eg"],
      consts["eg64"], consts["eb64"], consts["bd_gdn"])


def _lam_value(lamp_ref, lam_init):
    lp = lamp_ref[...]
    return (jnp.exp(jnp.sum(lp[0:1] * lp[1:2], axis=-1, keepdims=True))
            - jnp.exp(jnp.sum(lp[2:3] * lp[3:4], axis=-1, keepdims=True)) + lam_init)


def _softmax_step(carry, s, v16):
    m, l, acc = carry
    m_new = jnp.maximum(m, jnp.max(s, axis=-1, keepdims=True))
    alpha = jnp.exp(m - m_new)
    p = jnp.exp(s - m_new)
    l = alpha * l + jnp.sum(p, axis=-1, keepdims=True)
    acc = alpha * acc + _dot(p.astype(BF16), v16)
    return m_new, l, acc


def _diff_prompt_kernel(slope_ref, lamp_ref, on_ref, q_ref, k_ref, v_ref, o_ref, *, tq, lam_init):
    h = pl.program_id(1)
    qi = pl.program_id(2)
    slope = slope_ref[h]
    q = q_ref[...]
    lane = lax.broadcasted_iota(jnp.int32, (tq, LANES), 1)
    zero = jnp.zeros_like(q)
    qq = jnp.concatenate([jnp.where(lane < DK_DIFF, q, zero), jnp.where(lane >= DK_DIFF, q, zero)], axis=0)
    col = lax.broadcasted_iota(jnp.int32, (1, tq), 1)
    rowpos = lax.broadcasted_iota(jnp.int32, (2 * tq, 1), 0) & (tq - 1)

    def tile(kj, carry, masked):
        off = pl.multiple_of(kj * tq, tq)
        s = _dot_nt(qq, k_ref[pl.ds(off, tq), :])
        s = s + slope * (col + (kj - qi) * tq).astype(F32)
        if masked:
            s = jnp.where(col <= rowpos, s, NEG_BIG)
        return _softmax_step(carry, s, v_ref[pl.ds(off, tq), :])

    carry = (jnp.full((2 * tq, 1), NEG_BIG, F32), jnp.zeros((2 * tq, 1), F32), jnp.zeros((2 * tq, LANES), F32))
    carry = lax.fori_loop(0, qi, functools.partial(tile, masked=False), carry)
    _, l, acc = tile(qi, carry, True)
    o = acc / l
    d = o[0:tq] - _lam_value(lamp_ref, lam_init) * o[tq:2 * tq]
    o_ref[...] = (_rms_rows(d, on_ref[...]) * (1.0 - lam_init)).astype(BF16)


def _diff_prompt(q, k16, v16, slopes, lamp, onorm, bc, t, tq, lam_init):
    nq = t // tq
    return pl.pallas_call(
        functools.partial(_diff_prompt_kernel, tq=tq, lam_init=lam_init),
        grid=(bc, H_DIFF, nq),
        in_specs=[pl.BlockSpec(memory_space=pltpu.SMEM), _const_spec((4, DK_DIFF)), _const_spec((1, DV_DIFF)),
                  pl.BlockSpec((tq, LANES), lambda b, h, i: (b * nq + i, h)),
                  pl.BlockSpec((t, LANES), lambda b, h, i: (b, h)),
                  pl.BlockSpec((t, LANES), lambda b, h, i: (b, h))],
        out_specs=pl.BlockSpec((tq, LANES), lambda b, h, i: (b * nq + i, h)),
        out_shape=jax.ShapeDtypeStruct((bc * t, H_DIFF * DV_DIFF), BF16),
        compiler_params=_params(("parallel", "parallel", "arbitrary")),
        name="diff_prompt",
    )(slopes, lamp, onorm, q, k16, v16)


def _diff_decode_kernel(pt_ref, slope_ref, lamp_ref, on_ref, q_ref, kn_ref, vn_ref, *rest,
                        pages, rows, n_valid, past_len, lam_init):
    k_refs, v_refs = rest[:pages], rest[pages:2 * pages]
    o_ref, qq_s, m_s, l_s, acc_s = rest[2 * pages:]
    del pt_ref
    g = pl.program_id(1)
    row = lax.broadcasted_iota(jnp.int32, (rows, 1), 0)

    @pl.when(g == 0)
    def _():
        q = q_ref[...].astype(F32)
        lane = lax.broadcasted_iota(jnp.int32, q.shape, 1) & (LANES - 1)
        q1 = jnp.where(lane < DK_DIFF, q, 0.0)
        q2 = pltpu.roll(jnp.where(lane >= DK_DIFF, q, 0.0), n_valid, axis=0)
        qq = jnp.where(row < n_valid, q1, jnp.where(row < 2 * n_valid, q2, 0.0))
        qq_s[...] = qq.astype(BF16)
        m_s[...] = jnp.full(m_s.shape, NEG_BIG, F32)
        l_s[...] = jnp.zeros(l_s.shape, F32)
        acc_s[...] = jnp.zeros(acc_s.shape, F32)

    col = lax.broadcasted_iota(jnp.int32, (1, PAGE_SIZE), 1)
    for h in range(H_DIFF):
        hs = slice(h * LANES, (h + 1) * LANES)
        slope = slope_ref[h]
        qq = qq_s[:, hs]
        ss = []
        for i in range(pages):
            s = _dot_nt(qq, k_refs[i][:, h, :].astype(BF16))
            ss.append(s + slope * (col + ((g * pages + i) * PAGE_SIZE - past_len)).astype(F32))
        s_all = jnp.concatenate(ss, axis=1)
        m_prev = m_s[:, hs][:, 0:1]
        m_new = jnp.maximum(m_prev, jnp.max(s_all, axis=-1, keepdims=True))
        alpha = jnp.exp(m_prev - m_new)
        p = jnp.exp(s_all - m_new)
        l_new = alpha * l_s[:, hs][:, 0:1] + jnp.sum(p, axis=-1, keepdims=True)
        pv = jnp.zeros((rows, LANES), F32)
        for i in range(pages):
            pv = pv + _dot(p[:, i * PAGE_SIZE:(i + 1) * PAGE_SIZE].astype(BF16), v_refs[i][:, h, :].astype(BF16))
        m_s[:, hs] = jnp.broadcast_to(m_new, (rows, LANES))
        l_s[:, hs] = jnp.broadcast_to(l_new, (rows, LANES))
        acc_s[:, hs] = alpha * acc_s[:, hs] + pv

    @pl.when(g == pl.num_programs(1) - 1)
    def _():
        coln = lax.broadcasted_iota(jnp.int32, (1, rows), 1)
        tok = jnp.where(row < n_valid, row, row - n_valid)
        lam = _lam_value(lamp_ref, lam_init)
        for h in range(H_DIFF):
            hs = slice(h * LANES, (h + 1) * LANES)
            s = _dot_nt(qq_s[:, hs], kn_ref[:, hs]) + slope_ref[h] * coln.astype(F32)
            s = jnp.where((coln < n_valid) & (coln <= tok), s, NEG_BIG)
            carry = (m_s[:, hs][:, 0:1], l_s[:, hs][:, 0:1], acc_s[:, hs])
            _, l, acc = _softmax_step(carry, s, vn_ref[:, hs])
            o = acc / l
            d = o - lam * pltpu.roll(o, rows - n_valid, axis=0)
            d = _rms_rows(d, on_ref[...]) * (1.0 - lam_init)
            o_ref[:, hs] = jnp.where(row < n_valid, d, 0.0).astype(BF16)


def _diff_decode(q, kn16, vn16, cache_k, cache_v, page_table, slopes, lamp, onorm, layer, bc, rows, n_valid,
                 lam_init, pages):
    n_pages = page_table.shape[1]
    npg = n_pages // pages
    past_len = n_pages * PAGE_SIZE
    width = H_DIFF * LANES

    def page_spec(i):
        return pl.BlockSpec((None, None, PAGE_SIZE, H_DIFF, LANES),
                            lambda b, g, pt: (layer, pt[b, g * pages + i], 0, 0, 0))

    tile = pl.BlockSpec((rows, width), lambda b, g, pt: (b, 0))
    grid_spec = pltpu.PrefetchScalarGridSpec(
        num_scalar_prefetch=1,
        grid=(bc, npg),
        in_specs=([pl.BlockSpec(memory_space=pltpu.SMEM),
                   pl.BlockSpec((4, DK_DIFF), lambda b, g, pt: (0, 0)),
                   pl.BlockSpec((1, DV_DIFF), lambda b, g, pt: (0, 0)),
                   tile, tile, tile]
                  + [page_spec(i) for i in range(pages)] + [page_spec(i) for i in range(pages)]),
        out_specs=tile,
        scratch_shapes=[pltpu.VMEM((rows, width), BF16), pltpu.VMEM((rows, width), F32),
                        pltpu.VMEM((rows, width), F32), pltpu.VMEM((rows, width), F32)],
    )
    return pl.pallas_call(
        functools.partial(_diff_decode_kernel, pages=pages, rows=rows, n_valid=n_valid, past_len=past_len,
                          lam_init=lam_init),
        grid_spec=grid_spec,
        out_shape=jax.ShapeDtypeStruct((bc * rows, width), BF16),
        compiler_params=_params(("parallel", "arbitrary")),
        name="diff_decode",
    )(page_table, slopes, lamp, onorm, q, kn16, vn16, *([cache_k] * pages), *([cache_v] * pages))


def _ca_kernel(q_ref, k_ref, v_ref, o_ref):
    for h in range(H_CA):
        hs = slice(h * DH_CA, (h + 1) * DH_CA)
        s = _dot_nt(q_ref[:, hs], k_ref[:, h, :].astype(BF16))
        p = jnp.exp(s - jnp.max(s, axis=-1, keepdims=True))
        a = p / jnp.sum(p, axis=-1, keepdims=True)
        o_ref[:, hs] = _dot(a.astype(BF16), v_ref[:, h, :].astype(BF16)).astype(BF16)


def _ca_attn(q, mem_k, mem_v, layer, bc, t, tq):
    nq = t // tq
    mem_spec = pl.BlockSpec((None, None, N_MEM, H_CA, DH_CA), lambda b, i: (layer, b, 0, 0, 0))
    return pl.pallas_call(
        _ca_kernel,
        grid=(bc, nq),
        in_specs=[pl.BlockSpec((tq, D_MODEL), lambda b, i: (b * nq + i, 0)), mem_spec, mem_spec],
        out_specs=pl.BlockSpec((tq, D_MODEL), lambda b, i: (b * nq + i, 0)),
        out_shape=jax.ShapeDtypeStruct((bc * t, D_MODEL), BF16),
        compiler_params=_params(("parallel", "arbitrary")),
        name="ca_attn",
    )(q, mem_k, mem_v)


def _block_ones(n, blk):
    i = jnp.arange(n) // blk
    return (i[:, None] == i[None, :])


def _consts(chunk):
    hk, hv = H_GLA * DK_GLA, H_GLA * DV_GLA
    r = jnp.arange(LANES)
    c512 = jnp.arange(4 * LANES)
    c256 = jnp.arange(256)
    return {
        "tri": jnp.tril(jnp.ones((chunk, chunk), F32)).astype(BF16),
        "seg_gla": ((jnp.arange(hk) // DK_GLA)[:, None] == (jnp.arange(hv) // DV_GLA)[None, :]).astype(BF16),
        "bd_gla": ((jnp.arange(hk) // DK_GLA)[:, None] == (jnp.arange(hv) // DV_GLA)[None, :]).astype(F32),
        "p64_256": _block_ones(256, 64).astype(BF16),
        "bd_gdn": _block_ones(256, 64).astype(F32),
        "eg": (r[:, None] == (c512 // LANES)[None, :]).astype(BF16),
        "eg64": (r[:, None] == (c256 // 64)[None, :]).astype(BF16),
        "eb64": (r[:, None] == (c256 // 64 + H_GDN)[None, :]).astype(BF16),
    }


def _expand_state(s, bd):
    b, h, dk, dv = s.shape
    return jnp.tile(s.reshape(b, h * dk, dv), (1, 1, h)) * bd


def _compact_state(s, h):
    b, hk, hv = s.shape
    s5 = s.reshape(b, h, hk // h, h, hv // h)
    return jnp.stack([s5[:, i, :, i, :] for i in range(h)], axis=1)


def _layer(x, wts, layer, bc, t, chunk, n_valid, tm, s_gla, s_gdn, conv0, mem_k, mem_v, mem_layer, diff_fn,
           consts, p64_512):
    lam_init = 0.8 - 0.6 * math.exp(-0.3 * layer)
    gla, q16, k_rows, k16, v_rows, v16, gdn = _in_proj(x, wts["norm_mix"], wts["w_in"], wts["qk_gain"], p64_512, tm)
    o_gla, s_gla_new = _gla(gla, _expand_state(s_gla, consts["bd_gla"]), wts["gla_w_a2"], wts["gla_b_a"],
                            wts["gla_onorm"], consts, bc, t, chunk, n_valid)
    o_diff = diff_fn(q16, k16, v16, wts["lamp"], wts["diff_onorm"], lam_init)
    o_gdn, s_gdn_new, conv_new = _gdn(gdn, _expand_state(s_gdn, consts["bd_gdn"]), conv0, wts["gdn_conv_w"],
                                      wts["gdn_a_log"], wts["gdn_dt_bias"], wts["gdn_onorm"], consts, bc, t,
                                      chunk, n_valid)
    x = _mm_res([o_gla, o_diff, o_gdn], wts["w_out"], x, tm)
    qc = _norm_mm(x, wts["norm_ca"], wts["ca_wq"], wts["ca_qnorm"], tm, headnorm=True, scale=DH_CA ** -0.5,
                  out_dtype=BF16)
    oc = _ca_attn(qc, mem_k, mem_v, mem_layer, bc, t, min(t, 512))
    x = _mm_res([oc], [wts["ca_wo"]], x, tm)
    hid = _swiglu(x, wts["norm_ffn"], wts["ffn_w1"], wts["ffn_w3"], tm)
    x = _mm_res([hid], [wts["ffn_w2"]], x, tm)
    return (x, k_rows, v_rows, _compact_state(s_gla_new, H_GLA), _compact_state(s_gdn_new, H_GDN), conv_new)


def _pack_w_in(w):
    sizes = (128, 128, 256, 16, 256, 512, 512, 512, 256, 256, 256, 256, 4, 4)
    segs, off = [], 0
    for s in sizes:
        segs.append(w[:, off:off + s])
        off += s
    gq, gk, gv, ga, gg, dq, dk, dv, nq, nk, nv, nz, na, nb = segs
    z = lambda n: jnp.zeros((w.shape[0], n), w.dtype)
    packed = jnp.concatenate([gq, gk, gv, gg, ga, z(GLA_W - 784), dq, dk, dv, nq, nk, nv, nz, na, nb,
                              z(GDN_W - 1032)], axis=1)
    return packed.astype(BF16)


def _row(v, n=None):
    v = v.reshape(1, -1).astype(F32)
    if n is not None and v.shape[1] < n:
        v = jnp.concatenate([v, jnp.zeros((1, n - v.shape[1]), F32)], axis=1)
    return v


def _layer_weights(l, norm_mix, w_in, gla_w_a2, gla_b_a, gla_onorm, diff_qnorm, diff_knorm, lam_q1, lam_k1,
                   lam_q2, lam_k2, diff_onorm, gdn_conv_w, gdn_a_log, gdn_dt_bias, gdn_onorm, w_out, norm_ca,
                   ca_wq, ca_wo, ca_qnorm, norm_ffn, ffn_w1, ffn_w3, ffn_w2):
    wo = w_out[l].astype(BF16)
    a2 = jnp.concatenate([gla_w_a2[l], jnp.zeros((LANES - GLA_RANK, H_GLA * DK_GLA), F32)], axis=0).astype(BF16)
    n_q = 2 * H_DIFF
    return {
        "norm_mix": _row(norm_mix[l]), "w_in": _pack_w_in(w_in[l]),
        "qk_gain": jnp.concatenate([jnp.tile(_row(diff_qnorm[l]), (1, n_q)) * (DK_DIFF ** -0.5),
                                    jnp.tile(_row(diff_knorm[l]), (1, n_q))], axis=1),
        "gla_w_a2": a2, "gla_b_a": _row(gla_b_a[l]), "gla_onorm": jnp.tile(_row(gla_onorm[l]), (1, H_GLA)),
        "lamp": jnp.stack([lam_q1[l], lam_k1[l], lam_q2[l], lam_k2[l]]).astype(F32),
        "diff_onorm": _row(diff_onorm[l]),
        "gdn_conv_w": gdn_conv_w[l].astype(F32), "gdn_a_log": _row(gdn_a_log[l], LANES),
        "gdn_dt_bias": _row(gdn_dt_bias[l], LANES), "gdn_onorm": jnp.tile(_row(gdn_onorm[l]), (1, H_GDN)),
        "w_out": [wo[0:256], wo[256:768], wo[768:1024]],
        "norm_ca": _row(norm_ca[l]), "ca_wq": ca_wq[l].astype(BF16), "ca_wo": ca_wo[l].astype(BF16),
        "ca_qnorm": _row(ca_qnorm[l]), "norm_ffn": _row(norm_ffn[l]),
        "ffn_w1": ffn_w1[l].astype(BF16), "ffn_w3": ffn_w3[l].astype(BF16), "ffn_w2": ffn_w2[l].astype(BF16),
    }


def _forward(x_prompt, x_sample, mem_prompt, cache_k, cache_v, cache_mem_k, cache_mem_v, state_gla, state_gdn,
             state_conv, page_table, norm_mix, w_in, gla_w_a2, gla_b_a, gla_onorm, diff_qnorm, diff_knorm,
             lam_q1, lam_k1, lam_q2, lam_k2, diff_onorm, gdn_conv_w, gdn_a_log, gdn_dt_bias, gdn_onorm, w_out,
             norm_ca, norm_mem, ca_wq, ca_wk, ca_wv, ca_wo, ca_qnorm, ca_knorm, norm_ffn, ffn_w1, ffn_w3, ffn_w2,
             *, chunk_p, tm_p, tq_p, rows_s, pages):
    depth = w_in.shape[0]
    bp, tp, _ = x_prompt.shape
    bs, ts, _ = x_sample.shape
    n_mem = mem_prompt.shape[1]
    slopes = 2.0 ** (-8.0 * jnp.arange(1, H_DIFF + 1, dtype=F32) / H_DIFF)
    consts_p = _consts(chunk_p)
    consts_s = _consts(rows_s)
    p64_512 = _block_ones(DIFF_W, DK_DIFF).astype(BF16)

    xp = x_prompt.reshape(bp * tp, D_MODEL)
    xs = jnp.pad(x_sample, ((0, 0), (0, rows_s - ts), (0, 0))).reshape(bs * rows_s, D_MODEL)
    memp = mem_prompt.reshape(bp * n_mem, D_MODEL)
    zeros_gla = jnp.zeros((bp, H_GLA, DK_GLA, DV_GLA), F32)
    zeros_gdn = jnp.zeros((bp, H_GDN, DK_GDN, DV_GDN), F32)
    zeros_conv = jnp.zeros((bp, CONV_W - 1, C_CONV), F32)
    tm_m = min(bp * n_mem, 512)
    tm_s = bs * rows_s

    outs = [[] for _ in range(12)]
    for l in range(depth):
        wts = _layer_weights(l, norm_mix, w_in, gla_w_a2, gla_b_a, gla_onorm, diff_qnorm, diff_knorm, lam_q1,
                             lam_k1, lam_q2, lam_k2, diff_onorm, gdn_conv_w, gdn_a_log, gdn_dt_bias, gdn_onorm,
                             w_out, norm_ca, ca_wq, ca_wo, ca_qnorm, norm_ffn, ffn_w1, ffn_w3, ffn_w2)
        mk = _norm_mm(memp, _row(norm_mem[l]), ca_wk[l].astype(BF16), _row(ca_knorm[l]), tm_m, headnorm=True,
                      scale=1.0, out_dtype=F32).reshape(1, bp, n_mem, H_CA, DH_CA)
        mv = _norm_mm(memp, _row(norm_mem[l]), ca_wv[l].astype(BF16), _row(ca_knorm[l]), tm_m, headnorm=False,
                      scale=1.0, out_dtype=F32).reshape(1, bp, n_mem, H_CA, DH_CA)

        def diff_p(q16, k16, v16, lamp, onorm, lam_init):
            return _diff_prompt(q16, k16, v16, slopes, lamp, onorm, bp, tp, tq_p, lam_init)

        xp, kp, vp, sgp, sdp, cvp = _layer(xp, wts, l, bp, tp, chunk_p, chunk_p, tm_p, zeros_gla, zeros_gdn,
                                           zeros_conv, mk, mv, 0, diff_p, consts_p, p64_512)

        def diff_s(q16, k16, v16, lamp, onorm, lam_init, l=l):
            return _diff_decode(q16, k16, v16, cache_k, cache_v, page_table, slopes, lamp, onorm, l, bs, rows_s,
                                ts, lam_init, pages)

        xs, ks, vs, sgs, sds, cvs = _layer(xs, wts, l, bs, rows_s, rows_s, ts, tm_s, state_gla[l], state_gdn[l],
                                           state_conv[l], cache_mem_k, cache_mem_v, l, diff_s, consts_s, p64_512)
        vals = (kp.reshape(bp, tp, H_DIFF, 2 * DK_DIFF), vp.reshape(bp, tp, H_DIFF, DV_DIFF),
                ks.reshape(bs, rows_s, H_DIFF, 2 * DK_DIFF)[:, :ts], vs.reshape(bs, rows_s, H_DIFF, DV_DIFF)[:, :ts],
                sgp, sgs, sdp, sds, cvp, cvs, mk[0], mv[0])
        for o, v in zip(outs, vals):
            o.append(v)

    y_p = xp.reshape(bp, tp, D_MODEL)
    y_s = xs.reshape(bs, rows_s, D_MODEL)[:, :ts]
    return (y_p, y_s) + tuple(jnp.stack(o) for o in outs)


def kernel(x_prompt, x_sample, mem_prompt, cache_k, cache_v, cache_mem_k, cache_mem_v, state_gla, state_gdn, state_conv, page_table, norm_mix, w_in, gla_w_a2, gla_b_a, gla_onorm, diff_qnorm, diff_knorm, lam_q1, lam_k1, lam_q2, lam_k2, diff_onorm, gdn_conv_w, gdn_a_log, gdn_dt_bias, gdn_onorm, w_out, norm_ca, norm_mem, ca_wq, ca_wk, ca_wv, ca_wo, ca_qnorm, ca_knorm, norm_ffn, ffn_w1, ffn_w3, ffn_w2):
    return _forward(x_prompt, x_sample, mem_prompt, cache_k, cache_v, cache_mem_k, cache_mem_v, state_gla,
                    state_gdn, state_conv, page_table, norm_mix, w_in, gla_w_a2, gla_b_a, gla_onorm, diff_qnorm,
                    diff_knorm, lam_q1, lam_k1, lam_q2, lam_k2, diff_onorm, gdn_conv_w, gdn_a_log, gdn_dt_bias,
                    gdn_onorm, w_out, norm_ca, norm_mem, ca_wq, ca_wk, ca_wv, ca_wo, ca_qnorm, ca_knorm, norm_ffn,
                    ffn_w1, ffn_w3, ffn_w2, chunk_p=64, tm_p=512, tq_p=256, rows_s=16, pages=8)
```

```python
import functools
import math

import jax
import jax.numpy as jnp
from jax import lax
from jax.experimental import pallas as pl
from jax.experimental.pallas import tpu as pltpu

F32 = jnp.float32
BF16 = jnp.bfloat16

D_MODEL = 1024
DEPTH = 4
PAGE_SIZE = 128
H_GLA, DK_GLA, DV_GLA, GLA_RANK, GLA_TAU = 4, 32, 64, 16, 16.0
H_DIFF, DK_DIFF, DV_DIFF = 4, 64, 128
H_GDN, DK_GDN, DV_GDN, CONV_W = 4, 64, 64, 4
C_CONV = 2 * H_GDN * DK_GDN + H_GDN * DV_GDN
N_MEM, H_CA = 256, 4
DH_CA = D_MODEL // H_CA
D_FF = -(-8 * D_MODEL // (3 * 256)) * 256
RMS_EPS = 1e-6
L2_EPS = 1e-6

VMEM_LIMIT_V7X = 56 * 1024 * 1024
LANES = 128

GLA_W = 896
GDN_W = 1152
DIFF_W = 512
OFF_GLA, OFF_DQ, OFF_DK, OFF_DV, OFF_GDN = 0, 896, 1408, 1920, 2432
IN_W = OFF_GDN + GDN_W

NEG_BIG = -1e30


def _dot(a, b):
    return jnp.dot(a, b, preferred_element_type=F32)


def _dot_nt(a, b):
    return lax.dot_general(a, b, (((1,), (1,)), ((), ())), preferred_element_type=F32)


def _hilo(a):
    hi = a.astype(BF16)
    lo = (a - hi.astype(F32)).astype(BF16)
    return hi, lo


def _dot_sel_rhs(a, sel):
    hi, lo = _hilo(a)
    return _dot(hi, sel) + _dot(lo, sel)


def _dot_sel_lhs(sel, b):
    hi, lo = _hilo(b)
    return _dot(sel, hi) + _dot(sel, lo)


def _dot3(a, b):
    ah, al = _hilo(a)
    bh, bl = _hilo(b)
    return _dot(ah, bh) + _dot(ah, bl) + _dot(al, bh)


def _sigmoid(x):
    return 1.0 / (1.0 + jnp.exp(-x))


def _softplus(x):
    return jnp.maximum(x, 0.0) + jnp.log(1.0 + jnp.exp(-jnp.abs(x)))


def _rms_rows(x, g):
    ms = jnp.mean(x * x, axis=-1, keepdims=True)
    return x * lax.rsqrt(ms + RMS_EPS) * g


def _pad_rows_128(a):
    r = a.shape[0]
    if r == LANES:
        return a
    return jnp.concatenate([a, jnp.zeros((LANES - r, a.shape[1]), a.dtype)], axis=0)


def _const_spec(shape):
    nd = len(shape)
    return pl.BlockSpec(shape, lambda *_: (0,) * nd)


def _params(sem):
    return pltpu.CompilerParams(dimension_semantics=sem, vmem_limit_bytes=VMEM_LIMIT_V7X)


def _in_proj_kernel(x_ref, g_ref, w_ref, qkg_ref, p64_ref,
                    gla_ref, q_ref, k_ref, kb_ref, v_ref, vb_ref, vt_ref, gdn_ref):
    xn = _rms_rows(x_ref[...], g_ref[...]).astype(BF16)
    gla_ref[...] = _dot(xn, w_ref[:, OFF_GLA:OFF_GLA + GLA_W])

    def qknorm(y, gain):
        ss = _dot_sel_rhs(y * y, p64_ref[...])
        return y * lax.rsqrt(ss * (1.0 / DK_DIFF) + RMS_EPS) * gain

    yq = _dot(xn, w_ref[:, OFF_DQ:OFF_DQ + DIFF_W])
    q_ref[...] = qknorm(yq, qkg_ref[:, 0:DIFF_W]).astype(BF16)
    yk = _dot(xn, w_ref[:, OFF_DK:OFF_DK + DIFF_W])
    kk = qknorm(yk, qkg_ref[:, DIFF_W:2 * DIFF_W])
    k_ref[...] = kk
    kb_ref[...] = kk.astype(BF16)
    yv = _dot(xn, w_ref[:, OFF_DV:OFF_DV + DIFF_W])
    v_ref[...] = yv
    vb_ref[...] = yv.astype(BF16)
    vt_ref[...] = jnp.transpose(yv).astype(BF16)
    gdn_ref[...] = _dot(xn, w_ref[:, OFF_GDN:OFF_GDN + GDN_W])


def _in_proj(x, g, w, qkg, p64, tm):
    m = x.shape[0]
    row = lambda wd: pl.BlockSpec((tm, wd), lambda i: (i, 0))
    return pl.pallas_call(
        _in_proj_kernel,
        grid=(m // tm,),
        in_specs=[row(D_MODEL), _const_spec((1, D_MODEL)), _const_spec((D_MODEL, IN_W)),
                  _const_spec((1, 2 * DIFF_W)), _const_spec((DIFF_W, DIFF_W))],
        out_specs=[row(GLA_W), row(DIFF_W), row(DIFF_W), row(DIFF_W), row(DIFF_W), row(DIFF_W),
                   pl.BlockSpec((DIFF_W, tm), lambda i: (0, i)), row(GDN_W)],
        out_shape=[jax.ShapeDtypeStruct((m, GLA_W), F32), jax.ShapeDtypeStruct((m, DIFF_W), BF16),
                   jax.ShapeDtypeStruct((m, DIFF_W), F32), jax.ShapeDtypeStruct((m, DIFF_W), BF16),
                   jax.ShapeDtypeStruct((m, DIFF_W), F32), jax.ShapeDtypeStruct((m, DIFF_W), BF16),
                   jax.ShapeDtypeStruct((DIFF_W, m), BF16), jax.ShapeDtypeStruct((m, GDN_W), F32)],
        compiler_params=_params(("parallel",)),
        name="in_proj",
    )(x, g, w, qkg, p64)


def _norm_mm_kernel(x_ref, g_ref, w_ref, hg_ref, o_ref, *, headnorm, scale):
    xn = _rms_rows(x_ref[...], g_ref[...]).astype(BF16)
    y = _dot(xn, w_ref[...])
    if headnorm:
        for h in range(H_CA):
            yh = y[:, h * DH_CA:(h + 1) * DH_CA]
            o_ref[:, h * DH_CA:(h + 1) * DH_CA] = (_rms_rows(yh, hg_ref[...]) * scale).astype(o_ref.dtype)
    else:
        o_ref[...] = y.astype(o_ref.dtype)


def _norm_mm(x, g, w, hg, tm, *, headnorm, scale, out_dtype):
    m, n = x.shape[0], w.shape[1]
    return pl.pallas_call(
        functools.partial(_norm_mm_kernel, headnorm=headnorm, scale=scale),
        grid=(m // tm,),
        in_specs=[pl.BlockSpec((tm, D_MODEL), lambda i: (i, 0)), _const_spec((1, D_MODEL)),
                  _const_spec((D_MODEL, n)), _const_spec((1, DH_CA))],
        out_specs=pl.BlockSpec((tm, n), lambda i: (i, 0)),
        out_shape=jax.ShapeDtypeStruct((m, n), out_dtype),
        compiler_params=_params(("parallel",)),
        name="norm_mm",
    )(x, g, w, hg)


def _mm_res_kernel(*refs, n):
    xs, ws, res_ref, o_ref = refs[:n], refs[n:2 * n], refs[2 * n], refs[2 * n + 1]
    acc = res_ref[...]
    for x_ref, w_ref in zip(xs, ws):
        acc = acc + _dot(x_ref[...], w_ref[...])
    o_ref[...] = acc


def _mm_res(xs, ws, res, tm):
    m, n = res.shape
    k = len(xs)
    in_specs = ([pl.BlockSpec((tm, x.shape[1]), lambda i: (i, 0)) for x in xs]
                + [_const_spec(w.shape) for w in ws]
                + [pl.BlockSpec((tm, n), lambda i: (i, 0))])
    return pl.pallas_call(
        functools.partial(_mm_res_kernel, n=k),
        grid=(m // tm,),
        in_specs=in_specs,
        out_specs=pl.BlockSpec((tm, n), lambda i: (i, 0)),
        out_shape=jax.ShapeDtypeStruct((m, n), F32),
        compiler_params=_params(("parallel",)),
        name="mm_res",
    )(*xs, *ws, res)


FF_CHUNK = D_FF // 2


def _swiglu_kernel(x_ref, g_ref, w1_ref, w3_ref, o_ref):
    xn = _rms_rows(x_ref[...], g_ref[...]).astype(BF16)
    for c in range(D_FF // FF_CHUNK):
        sl = slice(c * FF_CHUNK, (c + 1) * FF_CHUNK)
        a = _dot(xn, w1_ref[:, sl])
        b = _dot(xn, w3_ref[:, sl])
        o_ref[:, sl] = (a * _sigmoid(a) * b).astype(BF16)


def _swiglu(x, g, w1, w3, tm):
    m = x.shape[0]
    return pl.pallas_call(
        _swiglu_kernel,
        grid=(m // tm,),
        in_specs=[pl.BlockSpec((tm, D_MODEL), lambda i: (i, 0)), _const_spec((1, D_MODEL)),
                  _const_spec((D_MODEL, D_FF)), _const_spec((D_MODEL, D_FF))],
        out_specs=pl.BlockSpec((tm, D_FF), lambda i: (i, 0)),
        out_shape=jax.ShapeDtypeStruct((m, D_FF), BF16),
        compiler_params=_params(("parallel",)),
        name="swiglu",
    )(x, g, w1, w3)


GLA_SB = 16


def _gla_chunk(p, s, wa_ref, ba_ref, on_ref, cum_ref, seg_ref, p64_ref, bd_ref, b_s, q_s, k_s, v_s, tall_s,
               *, chunk, n_valid):
    hk, hv = H_GLA * DK_GLA, H_GLA * DV_GLA
    nsb = chunk // GLA_SB
    q = p[:, 0:128] * (DK_GLA ** -0.5)
    k = p[:, 128:256]
    v = p[:, 256:512]
    gate = p[:, 512:768]
    x = _dot(p[:, 768:896].astype(BF16), wa_ref[...]) + ba_ref[...]
    la = (jnp.minimum(x, 0.0) - jnp.log(1.0 + jnp.exp(-jnp.abs(x)))) * (1.0 / GLA_TAU)
    if n_valid < chunk:
        rowv = lax.broadcasted_iota(jnp.int32, (chunk, 1), 0) < n_valid
        la = jnp.where(rowv, la, 0.0)
        k = jnp.where(rowv, k, 0.0)
    cums = _dot_sel_lhs(cum_ref[...], la)
    b = cums[0:chunk]
    b_s[...] = b
    q_s[...] = q
    k_s[...] = k
    v_s[...] = v
    v16 = v.astype(BF16)
    o = _dot((q * jnp.exp(b)).astype(BF16), s.astype(BF16))

    if nsb > 1:
        mq, mqc = cums[chunk:2 * chunk], cums[2 * chunk:3 * chunk]
        mh, mhc = cums[3 * chunk:4 * chunk], cums[4 * chunk:5 * chunk]
        lane = lax.broadcasted_iota(jnp.int32, (1, hk), 1)

        def stacked(a):
            return jnp.concatenate(
                [jnp.where((lane >= h * DK_GLA) & (lane < (h + 1) * DK_GLA), a, 0.0) for h in range(H_GLA)],
                axis=0).astype(BF16)

        a2 = _dot_nt(stacked(q * jnp.exp(b - mq)), (k * jnp.exp(mqc - b)).astype(BF16))
        a1 = _dot_nt(stacked(q * jnp.exp(b - mh)), (k * jnp.exp(mhc - b)).astype(BF16))
        qi = (lax.broadcasted_iota(jnp.int32, (H_GLA * chunk, chunk), 0) & (chunk - 1)) // GLA_SB
        qj = lax.broadcasted_iota(jnp.int32, (H_GLA * chunk, chunk), 1) // GLA_SB
        att = jnp.where(qi == qj + 1, a2, jnp.where(qi > qj + 1, a1, 0.0))
        r = _dot(att.astype(BF16), v16)
        lane_v = lax.broadcasted_iota(jnp.int32, (1, hv), 1)
        for h in range(H_GLA):
            o = o + jnp.where((lane_v >= h * DV_GLA) & (lane_v < (h + 1) * DV_GLA),
                              r[h * chunk:(h + 1) * chunk], 0.0)

    row = lax.broadcasted_iota(jnp.int32, (GLA_SB, 1), 0)
    for j in range(GLA_SB):
        for d in range(nsb):
            sl = slice(d * GLA_SB, (d + 1) * GLA_SB)
            bj = b_s[d * GLA_SB + j:d * GLA_SB + j + 1, :]
            kj = k_s[d * GLA_SB + j:d * GLA_SB + j + 1, :]
            e = jnp.exp(jnp.where(row >= j, b_s[sl, :] - bj, -jnp.inf))
            tall_s[j * chunk + d * GLA_SB:j * chunk + (d + 1) * GLA_SB, :] = (q_s[sl, :] * kj * e).astype(BF16)
    tsum = _dot(tall_s[...], seg_ref[...])
    for j in range(GLA_SB):
        vj = jnp.concatenate(
            [jnp.broadcast_to(v_s[d * GLA_SB + j:d * GLA_SB + j + 1, :], (GLA_SB, hv)) for d in range(nsb)], axis=0)
        o = o + tsum[j * chunk:(j + 1) * chunk] * vj

    b_end = b[chunk - 1:chunk, :]
    kd = k * jnp.exp(b_end - b)
    kd_t = jnp.transpose(_pad_rows_128(kd)).astype(BF16)
    upd = _dot(kd_t, _pad_rows_128(v).astype(BF16))
    dcol = jnp.transpose(jnp.broadcast_to(jnp.exp(b_end), (LANES, LANES)))[:, 0:1]
    s_new = s * dcol + upd * bd_ref[...]

    ss = _dot_sel_rhs(o * o, p64_ref[...])
    on = o * lax.rsqrt(ss * (1.0 / DV_GLA) + RMS_EPS) * on_ref[...]
    return on * gate * _sigmoid(gate), s_new


def _gla_kernel(p_ref, s0_ref, wa_ref, ba_ref, on_ref, cum_ref, seg_ref, p64_ref, bd_ref,
                o_ref, sout_ref, s_s, b_s, q_s, k_s, v_s, tall_s, *, chunk, n_chunks, n_valid):
    c = pl.program_id(1)

    @pl.when(c == 0)
    def _():
        s_s[...] = s0_ref[0]

    step = functools.partial(_gla_chunk, wa_ref=wa_ref, ba_ref=ba_ref, on_ref=on_ref, cum_ref=cum_ref,
                             seg_ref=seg_ref, p64_ref=p64_ref, bd_ref=bd_ref, b_s=b_s, q_s=q_s, k_s=k_s, v_s=v_s,
                             tall_s=tall_s, chunk=chunk, n_valid=n_valid)

    def body(ci, s):
        r0 = pl.multiple_of(ci * chunk, chunk)
        o, s_new = step(p_ref[pl.ds(r0, chunk), :], s)
        o_ref[pl.ds(r0, chunk), :] = o.astype(BF16)
        return s_new

    s_new = lax.fori_loop(0, n_chunks, body, s_s[...])
    s_s[...] = s_new

    @pl.when(c == pl.num_programs(1) - 1)
    def _():
        sout_ref[0] = s_new


def _gla(proj, s0, wa, ba, onorm, consts, bc, t, chunk, n_valid, block):
    nb = t // block
    hk, hv = H_GLA * DK_GLA, H_GLA * DV_GLA
    cum = consts["gla_cum"]
    return pl.pallas_call(
        functools.partial(_gla_kernel, chunk=chunk, n_chunks=block // chunk, n_valid=n_valid),
        grid=(bc, nb),
        in_specs=[pl.BlockSpec((block, GLA_W), lambda b, c: (b * nb + c, 0)),
                  pl.BlockSpec((1, hk, hv), lambda b, c: (b, 0, 0)),
                  _const_spec((LANES, hk)), _const_spec((1, hk)), _const_spec((1, hv)),
                  _const_spec(cum.shape), _const_spec((hk, hv)), _const_spec((hv, hv)),
                  _const_spec((hk, hv))],
        out_specs=[pl.BlockSpec((block, hv), lambda b, c: (b * nb + c, 0)),
                   pl.BlockSpec((1, hk, hv), lambda b, c: (b, 0, 0))],
        out_shape=[jax.ShapeDtypeStruct((bc * t, hv), BF16), jax.ShapeDtypeStruct((bc, hk, hv), F32)],
        scratch_shapes=[pltpu.VMEM((hk, hv), F32), pltpu.VMEM((chunk, hk), F32), pltpu.VMEM((chunk, hk), F32),
                        pltpu.VMEM((chunk, hk), F32), pltpu.VMEM((chunk, hv), F32),
                        pltpu.VMEM((GLA_SB * chunk, hk), BF16)],
        compiler_params=_params(("parallel", "arbitrary")),
        name="gla",
    )(proj, s0, wa, ba, onorm, cum, consts["seg_gla"], consts["p64_256"], consts["bd_gla"])


def _gdn_kernel(p_ref, s0_ref, cb_ref, cw_ref, alog_ref, dtb_ref, on_ref,
                tri_ref, p64_ref, eg_ref, eg64_ref, eb64_ref, bd_ref,
                o_ref, sout_ref, cout_ref, s_s, xp_s, *, chunk, n_valid):
    c = pl.program_id(1)
    hd = H_GDN * DK_GDN

    @pl.when(c == 0)
    def _():
        s_s[...] = s0_ref[0]
        xp_s[5:8, :] = cb_ref[0]

    p = p_ref[...]
    xp_s[8:8 + chunk, :] = p[:, 0:C_CONV]
    y = xp_s[pl.ds(5, chunk), :] * cw_ref[0:1, :]
    for i in range(1, CONV_W):
        y = y + xp_s[pl.ds(5 + i, chunk), :] * cw_ref[i:i + 1, :]
    y = y * _sigmoid(y)
    tail = xp_s[pl.ds(5 + n_valid, CONV_W - 1), :]
    xp_s[5:8, :] = tail

    @pl.when(c == pl.num_programs(1) - 1)
    def _():
        cout_ref[0] = tail

    def l2n(a):
        ss = _dot_sel_rhs(a * a, p64_ref[...])
        return a * lax.rsqrt(ss + L2_EPS)

    cq = l2n(y[:, 0:hd]) * (DK_GDN ** -0.5)
    ck = l2n(y[:, hd:2 * hd])
    cv = y[:, 2 * hd:3 * hd]
    z = p[:, 768:1024]
    small = p[:, 1024:1152]
    g = -jnp.exp(alog_ref[...]) * _softplus(small + dtb_ref[...])
    beta = _sigmoid(small)
    if n_valid < chunk:
        rowv = lax.broadcasted_iota(jnp.int32, (chunk, 1), 0) < n_valid
        g = jnp.where(rowv, g, 0.0)
        beta = jnp.where(rowv, beta, 0.0)
    gam = _dot_sel_lhs(tri_ref[...], g)
    gam_w = _dot_sel_rhs(gam, eg_ref[...])
    gam_b = _dot_sel_rhs(gam, eg64_ref[...])
    beta_b = _dot_sel_rhs(beta, eb64_ref[...])
    gam_t = jnp.transpose(_pad_rows_128(gam))

    eg = jnp.exp(gam_b)
    kb = ck * beta_b
    kbg = kb * eg
    vb = cv * beta_b
    g_end = gam_b[chunk - 1:chunk, :]
    kdec = ck * jnp.exp(g_end - gam_b)
    s = s_s[...]
    s16 = s.astype(BF16)
    ck16 = ck.astype(BF16)

    ri = lax.broadcasted_iota(jnp.int32, (chunk, chunk), 0)
    ci = lax.broadcasted_iota(jnp.int32, (chunk, chunk), 1)
    eye = (ri == ci).astype(F32)
    lane = lax.broadcasted_iota(jnp.int32, (1, hd), 1)
    n_sq = int(math.log2(chunk)) - 1

    heads = range(H_GDN)
    lms = [(lane >= h * DK_GDN) & (lane < (h + 1) * DK_GDN) for h in heads]
    decs = [jnp.exp(jnp.where(ri >= ci, gam_w[:, h * LANES:h * LANES + chunk] - gam_t[h:h + 1, 0:chunk], -jnp.inf))
            for h in heads]
    pws = [-(_dot_nt(jnp.where(lms[h], kb, 0.0).astype(BF16), ck16) * jnp.where(ri > ci, decs[h], 0.0))
           for h in heads]
    atts = [(_dot_nt(jnp.where(lms[h], cq, 0.0).astype(BF16), ck16) * decs[h]).astype(BF16) for h in heads]
    tinvs = [eye + pws[h] for h in heads]
    for _ in range(n_sq):
        pws = [_dot3(pws[h], pws[h]) for h in heads]
        tinvs = [tinvs[h] + _dot3(tinvs[h], pws[h]) for h in heads]
    sols = [_dot3(tinvs[h], jnp.concatenate([jnp.where(lms[h], vb, 0.0), jnp.where(lms[h], kbg, 0.0)], axis=1))
            for h in heads]
    value = sols[0][:, 0:hd] + sols[1][:, 0:hd] + sols[2][:, 0:hd] + sols[3][:, 0:hd]
    kcum = sols[0][:, hd:2 * hd] + sols[1][:, hd:2 * hd] + sols[2][:, hd:2 * hd] + sols[3][:, hd:2 * hd]

    v_new = value - _dot(kcum.astype(BF16), s16)
    o = _dot((cq * eg).astype(BF16), s16)
    for h in heads:
        o = o + _dot(atts[h], jnp.where(lms[h], v_new, 0.0).astype(BF16))
    kdec_t = jnp.transpose(_pad_rows_128(kdec)).astype(BF16)
    upd = _dot(kdec_t, _pad_rows_128(v_new).astype(BF16))
    s_new = s * jnp.exp(g_end) + upd * bd_ref[...]
    s_s[...] = s_new

    ss = _dot_sel_rhs(o * o, p64_ref[...])
    on = o * lax.rsqrt(ss * (1.0 / DV_GDN) + RMS_EPS) * on_ref[...]
    o_ref[...] = (on * z * _sigmoid(z)).astype(BF16)

    @pl.when(c == pl.num_programs(1) - 1)
    def _():
        sout_ref[0] = s_new


def _gdn(proj, s0, conv0, cw, alog, dtb, onorm, consts, bc, t, chunk, n_valid):
    nc = t // chunk
    hd = H_GDN * DK_GDN
    return pl.pallas_call(
        functools.partial(_gdn_kernel, chunk=chunk, n_valid=n_valid),
        grid=(bc, nc),
        in_specs=[pl.BlockSpec((chunk, GDN_W), lambda b, c: (b * nc + c, 0)),
                  pl.BlockSpec((1, hd, hd), lambda b, c: (b, 0, 0)),
                  pl.BlockSpec((1, CONV_W - 1, C_CONV), lambda b, c: (b, 0, 0)),
                  _const_spec((CONV_W, C_CONV)), _const_spec((1, LANES)), _const_spec((1, LANES)),
                  _const_spec((1, hd)),
                  _const_spec((chunk, chunk)), _const_spec((hd, hd)), _const_spec((LANES, 4 * LANES)),
                  _const_spec((LANES, hd)), _const_spec((LANES, hd)), _const_spec((hd, hd))],
        out_specs=[pl.BlockSpec((chunk, hd), lambda b, c: (b * nc + c, 0)),
                   pl.BlockSpec((1, hd, hd), lambda b, c: (b, 0, 0)),
                   pl.BlockSpec((1, CONV_W - 1, C_CONV), lambda b, c: (b, 0, 0))],
        out_shape=[jax.ShapeDtypeStruct((bc * t, hd), BF16), jax.ShapeDtypeStruct((bc, hd, hd), F32),
                   jax.ShapeDtypeStruct((bc, CONV_W - 1, C_CONV), F32)],
        scratch_shapes=[pltpu.VMEM((hd, hd), F32), pltpu.VMEM((chunk + 8, C_CONV), F32)],
        compiler_params=_params(("parallel", "arbitrary")),
        name="gdn",
    )(proj, s0, conv0, cw, alog, dtb, onorm, consts["tri"], consts["p64_256"], consts["eg"],
      consts["eg64"], consts["eb64"], consts["bd_gdn"])


def _lam_value(lamp_ref, lam_init):
    lp = lamp_ref[...]
    return (jnp.exp(jnp.sum(lp[0:1] * lp[1:2], axis=-1, keepdims=True))
            - jnp.exp(jnp.sum(lp[2:3] * lp[3:4], axis=-1, keepdims=True)) + lam_init)


def _softmax_step(carry, s, v16):
    m, l, acc = carry
    m_new = jnp.maximum(m, jnp.max(s, axis=-1, keepdims=True))
    alpha = jnp.exp(m - m_new)
    p = jnp.exp(s - m_new)
    l = alpha * l + jnp.sum(p, axis=-1, keepdims=True)
    acc = alpha * acc + _dot(p.astype(BF16), v16)
    return m_new, l, acc


ATT_SUB = 128


def _diff_prompt_kernel(slope_ref, lamp_ref, ong_ref, q_ref, k_ref, vt_ref, o_ref, *, tq, lam_init):
    h = pl.program_id(1)
    qi = pl.program_id(2)
    sub = ATT_SUB
    nsub = tq // sub
    w = nsub * 2 * sub
    slope = slope_ref[h]
    lam = _lam_value(lamp_ref, lam_init)
    base = slope * lax.broadcasted_iota(jnp.int32, (tq, w), 0).astype(F32)
    rowc = lax.broadcasted_iota(jnp.int32, (LANES, sub), 0)
    lane_t = lax.broadcasted_iota(jnp.int32, (1, w), 1) // (2 * sub)
    coff = -slope * (lane_t * sub).astype(F32)

    qts = []
    for t in range(nsub):
        q_t = jnp.transpose(q_ref[t * sub:(t + 1) * sub, :].astype(F32))
        qts += [jnp.where(rowc < DK_DIFF, q_t, 0.0), jnp.where(rowc >= DK_DIFF, q_t, 0.0)]
    qt = jnp.concatenate(qts, axis=1).astype(BF16)

    def update(carry, s, c, vt):
        m, l, acc = carry
        m_new = jnp.maximum(m, jnp.max(s, axis=0, keepdims=True) + c)
        p = jnp.exp(s - (m_new - c))
        alpha = jnp.exp(m - m_new)
        l = alpha * l + jnp.sum(p, axis=0, keepdims=True)
        acc = alpha * acc + _dot(vt, p.astype(BF16))
        return m_new, l, acc

    def full_block(kj, carry):
        off = pl.multiple_of(kj * tq, tq)
        s = _dot(k_ref[pl.ds(off, tq), :], qt) + base
        c = slope * ((kj - qi) * tq).astype(F32) + coff
        return update(carry, s, c, vt_ref[:, pl.ds(off, tq)])

    carry = (jnp.full((1, w), NEG_BIG, F32), jnp.zeros((1, w), F32), jnp.zeros((LANES, w), F32))
    m, l, acc = lax.fori_loop(0, qi, full_block, carry)
    off = pl.multiple_of(qi * tq, tq)
    for t in range(nsub):
        ls = slice(t * 2 * sub, (t + 1) * 2 * sub)
        nk = (t + 1) * sub
        kr = lax.broadcasted_iota(jnp.int32, (nk, 2 * sub), 0)
        qc = lax.broadcasted_iota(jnp.int32, (nk, 2 * sub), 1) & (sub - 1)
        s = _dot(k_ref[pl.ds(off, nk), :], qt[:, ls]) + slope * kr.astype(F32)
        s = jnp.where(kr <= qc + t * sub, s, NEG_BIG)
        _, l_t, acc_t = update((m[:, ls], l[:, ls], acc[:, ls]), s, -slope * float(t * sub),
                               vt_ref[:, pl.ds(off, nk)])
        o = acc_t / l_t
        d = o[:, 0:sub] - lam * o[:, sub:2 * sub]
        ms = jnp.mean(d * d, axis=0, keepdims=True)
        dn = d * lax.rsqrt(ms + RMS_EPS) * ong_ref[:, 0:sub] * (1.0 - lam_init)
        o_ref[t * sub:(t + 1) * sub, :] = jnp.transpose(dn).astype(BF16)


def _diff_prompt(q, k16, vt16, slopes, lamp, onorm, bc, t, tq, lam_init):
    nq = t // tq
    assert tq % ATT_SUB == 0 and DV_DIFF == LANES
    ong = jnp.broadcast_to(onorm.reshape(DV_DIFF, 1), (DV_DIFF, LANES))
    return pl.pallas_call(
        functools.partial(_diff_prompt_kernel, tq=tq, lam_init=lam_init),
        grid=(bc, H_DIFF, nq),
        in_specs=[pl.BlockSpec(memory_space=pltpu.SMEM), _const_spec((4, DK_DIFF)), _const_spec((DV_DIFF, LANES)),
                  pl.BlockSpec((tq, LANES), lambda b, h, i: (b * nq + i, h)),
                  pl.BlockSpec((t, LANES), lambda b, h, i: (b, h)),
                  pl.BlockSpec((LANES, t), lambda b, h, i: (h, b))],
        out_specs=pl.BlockSpec((tq, LANES), lambda b, h, i: (b * nq + i, h)),
        out_shape=jax.ShapeDtypeStruct((bc * t, H_DIFF * DV_DIFF), BF16),
        compiler_params=_params(("parallel", "parallel", "arbitrary")),
        name="diff_prompt",
    )(slopes, lamp, ong, q, k16, vt16)


DEC_GROUP = 4


def _diff_decode_kernel(pt_ref, lamp_ref, on_ref, scol_ref, srow_ref, q_ref, kn_ref, vn_ref, *rest,
                        pages, rows, n_valid, past_len, lam_init):
    k_refs, v_refs = rest[:pages], rest[pages:2 * pages]
    o_ref, qq_s, m_s, l_s, acc_s = rest[2 * pages:]
    del pt_ref
    g = pl.program_id(1)
    qr = 2 * n_valid
    nr = H_DIFF * qr
    prow = PAGE_SIZE * H_DIFF
    rowh = lax.broadcasted_iota(jnp.int32, (nr, 1), 0) // qr

    @pl.when(g == 0)
    def _():
        row = lax.broadcasted_iota(jnp.int32, (rows, 1), 0)
        lane = lax.broadcasted_iota(jnp.int32, (rows, LANES), 1)
        for h in range(H_DIFF):
            q = q_ref[:, h * LANES:(h + 1) * LANES].astype(F32)
            q1 = jnp.where(lane < DK_DIFF, q, 0.0)
            q2 = pltpu.roll(jnp.where(lane >= DK_DIFF, q, 0.0), n_valid, axis=0)
            qq_s[h * qr:(h + 1) * qr, :] = jnp.where(row < n_valid, q1, q2)[0:qr]
        m_s[...] = jnp.full(m_s.shape, NEG_BIG, F32)
        l_s[...] = jnp.zeros(l_s.shape, F32)
        acc_s[...] = jnp.zeros(acc_s.shape, F32)

    qq = qq_s[...].astype(BF16)
    col = lax.broadcasted_iota(jnp.int32, (1, prow), 1)
    tcol = (col // H_DIFF).astype(F32)
    own = rowh == (col & (H_DIFF - 1))
    scol = scol_ref[...]
    groups = range(0, pages, DEC_GROUP)
    s_g = []
    for i0 in groups:
        ss = []
        for i in range(i0, i0 + DEC_GROUP):
            s = _dot_nt(qq, k_refs[i][...].astype(BF16))
            base = ((g * pages + i) * PAGE_SIZE - past_len).astype(F32)
            ss.append(jnp.where(own, s + scol * (tcol + base), NEG_BIG))
        s_g.append(jnp.concatenate(ss, axis=1))
    m_g = [jnp.max(s, axis=-1, keepdims=True) for s in s_g]
    p_g = [jnp.exp(s - m) for s, m in zip(s_g, m_g)]
    l_g = [jnp.sum(p, axis=-1, keepdims=True) for p in p_g]
    pv_g = []
    for i0, p32 in zip(groups, p_g):
        p = p32.astype(BF16)
        pv = _dot(p[:, 0:prow], v_refs[i0][...].astype(BF16))
        for i in range(1, DEC_GROUP):
            pv = pv + _dot(p[:, i * prow:(i + 1) * prow], v_refs[i0 + i][...].astype(BF16))
        pv_g.append(pv)
    m_prev = m_s[:, 0:1]
    m_new = m_prev
    for m in m_g:
        m_new = jnp.maximum(m_new, m)
    alpha = jnp.exp(m_prev - m_new)
    l_new = alpha * l_s[:, 0:1]
    acc_new = alpha * acc_s[...]
    for m, l, pv in zip(m_g, l_g, pv_g):
        a = jnp.exp(m - m_new)
        l_new = l_new + a * l
        acc_new = acc_new + a * pv
    m_s[...] = jnp.broadcast_to(m_new, m_s.shape)
    l_s[...] = jnp.broadcast_to(l_new, l_s.shape)
    acc_s[...] = acc_new

    @pl.when(g == pl.num_programs(1) - 1)
    def _():
        coln = lax.broadcasted_iota(jnp.int32, (1, rows), 1)
        tok = lax.broadcasted_iota(jnp.int32, (nr, 1), 0) & (n_valid - 1)
        s = jnp.zeros((nr, rows), F32)
        for h in range(H_DIFF):
            s = jnp.where(rowh == h, _dot_nt(qq, kn_ref[:, h * LANES:(h + 1) * LANES]), s)
        s = s + srow_ref[...] * coln.astype(F32)
        s = jnp.where((coln < n_valid) & (coln <= tok), s, NEG_BIG)
        m_fin = jnp.maximum(m_new, jnp.max(s, axis=-1, keepdims=True))
        alpha = jnp.exp(m_new - m_fin)
        p32 = jnp.exp(s - m_fin)
        l_fin = alpha * l_new + jnp.sum(p32, axis=-1, keepdims=True)
        p = p32.astype(BF16)
        acc = alpha * acc_new
        for h in range(H_DIFF):
            acc = acc + jnp.where(rowh == h, _dot(p, vn_ref[:, h * LANES:(h + 1) * LANES]), 0.0)
        o = acc / l_fin
        d = o - _lam_value(lamp_ref, lam_init) * pltpu.roll(o, nr - n_valid, axis=0)
        d = _rms_rows(d, on_ref[...]) * (1.0 - lam_init)
        row8 = lax.broadcasted_iota(jnp.int32, (qr, 1), 0)
        for h in range(H_DIFF):
            blk = jnp.where(row8 < n_valid, d[h * qr:(h + 1) * qr], 0.0)
            o_ref[:, h * LANES:(h + 1) * LANES] = jnp.concatenate(
                [blk, jnp.zeros((rows - qr, LANES), F32)], axis=0).astype(BF16)


def _diff_decode(q, kn16, vn16, cache_k, cache_v, page_table, slopes, lamp, onorm, layer, bc, rows, n_valid,
                 lam_init, pages):
    n_pages = page_table.shape[1]
    npg = n_pages // pages
    past_len = n_pages * PAGE_SIZE
    width = H_DIFF * LANES
    prow = PAGE_SIZE * H_DIFF
    nr = H_DIFF * 2 * n_valid
    assert pages % DEC_GROUP == 0 and n_valid & (n_valid - 1) == 0 and 2 * n_valid <= rows
    scol = jnp.tile(slopes, PAGE_SIZE).reshape(1, prow)
    srow = jnp.repeat(slopes, 2 * n_valid).reshape(nr, 1)

    def page_spec(i):
        return pl.BlockSpec((None, None, prow, LANES), lambda b, g, pt: (layer, pt[b, g * pages + i], 0, 0))

    tile = pl.BlockSpec((rows, width), lambda b, g, pt: (b, 0))
    grid_spec = pltpu.PrefetchScalarGridSpec(
        num_scalar_prefetch=1,
        grid=(bc, npg),
        in_specs=([pl.BlockSpec((4, DK_DIFF), lambda b, g, pt: (0, 0)),
                   pl.BlockSpec((1, DV_DIFF), lambda b, g, pt: (0, 0)),
                   pl.BlockSpec((1, prow), lambda b, g, pt: (0, 0)),
                   pl.BlockSpec((nr, 1), lambda b, g, pt: (0, 0)),
                   tile, tile, tile]
                  + [page_spec(i) for i in range(pages)] + [page_spec(i) for i in range(pages)]),
        out_specs=tile,
        scratch_shapes=[pltpu.VMEM((nr, LANES), F32), pltpu.VMEM((nr, LANES), F32),
                        pltpu.VMEM((nr, LANES), F32), pltpu.VMEM((nr, LANES), F32)],
    )
    return pl.pallas_call(
        functools.partial(_diff_decode_kernel, pages=pages, rows=rows, n_valid=n_valid, past_len=past_len,
                          lam_init=lam_init),
        grid_spec=grid_spec,
        out_shape=jax.ShapeDtypeStruct((bc * rows, width), BF16),
        compiler_params=_params(("parallel", "arbitrary")),
        name="diff_decode",
    )(page_table, lamp, onorm, scol, srow, q, kn16, vn16, *([cache_k] * pages), *([cache_v] * pages))


def _ca_kernel(q_ref, k_ref, v_ref, o_ref):
    for h in range(H_CA):
        hs = slice(h * DH_CA, (h + 1) * DH_CA)
        s = _dot_nt(q_ref[:, hs], k_ref[:, h, :].astype(BF16))
        p = jnp.exp(s - jnp.max(s, axis=-1, keepdims=True))
        a = p / jnp.sum(p, axis=-1, keepdims=True)
        o_ref[:, hs] = _dot(a.astype(BF16), v_ref[:, h, :].astype(BF16)).astype(BF16)


def _ca_attn(q, mem_k, mem_v, layer, bc, t, tq):
    nq = t // tq
    mem_spec = pl.BlockSpec((None, None, N_MEM, H_CA, DH_CA), lambda b, i: (layer, b, 0, 0, 0))
    return pl.pallas_call(
        _ca_kernel,
        grid=(bc, nq),
        in_specs=[pl.BlockSpec((tq, D_MODEL), lambda b, i: (b * nq + i, 0)), mem_spec, mem_spec],
        out_specs=pl.BlockSpec((tq, D_MODEL), lambda b, i: (b * nq + i, 0)),
        out_shape=jax.ShapeDtypeStruct((bc * t, D_MODEL), BF16),
        compiler_params=_params(("parallel", "arbitrary")),
        name="ca_attn",
    )(q, mem_k, mem_v)


def _block_ones(n, blk):
    i = jnp.arange(n) // blk
    return (i[:, None] == i[None, :])


def _consts(chunk):
    hk, hv = H_GLA * DK_GLA, H_GLA * DV_GLA
    r = jnp.arange(LANES)
    c512 = jnp.arange(4 * LANES)
    c256 = jnp.arange(256)
    i = jnp.arange(chunk)[:, None]
    t = jnp.arange(chunk)[None, :]
    cums = [t <= i]
    if chunk > GLA_SB:
        assert chunk == 4 * GLA_SB
        half = chunk // 2
        cums += [t < GLA_SB * (i // GLA_SB), t <= GLA_SB * (i // GLA_SB) + GLA_SB - 1,
                 t < half * (i // half), t <= half * (i // half) + half - 1]
    return {
        "gla_cum": jnp.concatenate(cums, axis=0).astype(BF16),
        "tri": jnp.tril(jnp.ones((chunk, chunk), F32)).astype(BF16),
        "seg_gla": ((jnp.arange(hk) // DK_GLA)[:, None] == (jnp.arange(hv) // DV_GLA)[None, :]).astype(BF16),
        "bd_gla": ((jnp.arange(hk) // DK_GLA)[:, None] == (jnp.arange(hv) // DV_GLA)[None, :]).astype(F32),
        "p64_256": _block_ones(256, 64).astype(BF16),
        "bd_gdn": _block_ones(256, 64).astype(F32),
        "eg": (r[:, None] == (c512 // LANES)[None, :]).astype(BF16),
        "eg64": (r[:, None] == (c256 // 64)[None, :]).astype(BF16),
        "eb64": (r[:, None] == (c256 // 64 + H_GDN)[None, :]).astype(BF16),
    }


def _expand_state(s, bd):
    b, h, dk, dv = s.shape
    return jnp.tile(s.reshape(b, h * dk, dv), (1, 1, h)) * bd


def _compact_state(s, h):
    b, hk, hv = s.shape
    s5 = s.reshape(b, h, hk // h, h, hv // h)
    return jnp.stack([s5[:, i, :, i, :] for i in range(h)], axis=1)


def _layer(x, wts, layer, bc, t, chunk, n_valid, tm, s_gla, s_gdn, conv0, mem_k, mem_v, mem_layer, diff_fn,
           consts, p64_512):
    lam_init = 0.8 - 0.6 * math.exp(-0.3 * layer)
    gla, q16, k_rows, k16, v_rows, v16, vt16, gdn = _in_proj(x, wts["norm_mix"], wts["w_in"], wts["qk_gain"], p64_512, tm)
    o_gla, s_gla_new = _gla(gla, _expand_state(s_gla, consts["bd_gla"]), wts["gla_w_a2"], wts["gla_b_a"],
                            wts["gla_onorm"], consts, bc, t, chunk, n_valid, min(t, 256))
    o_diff = diff_fn(q16, k16, v16, vt16, wts["lamp"], wts["diff_onorm"], lam_init)
    o_gdn, s_gdn_new, conv_new = _gdn(gdn, _expand_state(s_gdn, consts["bd_gdn"]), conv0, wts["gdn_conv_w"],
                                      wts["gdn_a_log"], wts["gdn_dt_bias"], wts["gdn_onorm"], consts, bc, t,
                                      chunk, n_valid)
    x = _mm_res([o_gla, o_diff, o_gdn], wts["w_out"], x, tm)
    qc = _norm_mm(x, wts["norm_ca"], wts["ca_wq"], wts["ca_qnorm"], tm, headnorm=True, scale=DH_CA ** -0.5,
                  out_dtype=BF16)
    oc = _ca_attn(qc, mem_k, mem_v, mem_layer, bc, t, min(t, 512))
    x = _mm_res([oc], [wts["ca_wo"]], x, tm)
    hid = _swiglu(x, wts["norm_ffn"], wts["ffn_w1"], wts["ffn_w3"], tm)
    x = _mm_res([hid], [wts["ffn_w2"]], x, tm)
    return (x, k_rows, v_rows, _compact_state(s_gla_new, H_GLA), _compact_state(s_gdn_new, H_GDN), conv_new)


def _pack_w_in(w):
    sizes = (128, 128, 256, 16, 256, 512, 512, 512, 256, 256, 256, 256, 4, 4)
    segs, off = [], 0
    for s in sizes:
        segs.append(w[:, off:off + s])
        off += s
    gq, gk, gv, ga, gg, dq, dk, dv, nq, nk, nv, nz, na, nb = segs
    z = lambda n: jnp.zeros((w.shape[0], n), w.dtype)
    packed = jnp.concatenate([gq, gk, gv, gg, ga, z(GLA_W - 784), dq, dk, dv, nq, nk, nv, nz, na, nb,
                              z(GDN_W - 1032)], axis=1)
    return packed.astype(BF16)


def _row(v, n=None):
    v = v.reshape(1, -1).astype(F32)
    if n is not None and v.shape[1] < n:
        v = jnp.concatenate([v, jnp.zeros((1, n - v.shape[1]), F32)], axis=1)
    return v


def _layer_weights(l, norm_mix, w_in, gla_w_a2, gla_b_a, gla_onorm, diff_qnorm, diff_knorm, lam_q1, lam_k1,
                   lam_q2, lam_k2, diff_onorm, gdn_conv_w, gdn_a_log, gdn_dt_bias, gdn_onorm, w_out, norm_ca,
                   ca_wq, ca_wo, ca_qnorm, norm_ffn, ffn_w1, ffn_w3, ffn_w2):
    wo = w_out[l].astype(BF16)
    a2 = jnp.concatenate([gla_w_a2[l], jnp.zeros((LANES - GLA_RANK, H_GLA * DK_GLA), F32)], axis=0).astype(BF16)
    n_q = 2 * H_DIFF
    return {
        "norm_mix": _row(norm_mix[l]), "w_in": _pack_w_in(w_in[l]),
        "qk_gain": jnp.concatenate([jnp.tile(_row(diff_qnorm[l]), (1, n_q)) * (DK_DIFF ** -0.5),
                                    jnp.tile(_row(diff_knorm[l]), (1, n_q))], axis=1),
        "gla_w_a2": a2, "gla_b_a": _row(gla_b_a[l]), "gla_onorm": jnp.tile(_row(gla_onorm[l]), (1, H_GLA)),
        "lamp": jnp.stack([lam_q1[l], lam_k1[l], lam_q2[l], lam_k2[l]]).astype(F32),
        "diff_onorm": _row(diff_onorm[l]),
        "gdn_conv_w": gdn_conv_w[l].astype(F32), "gdn_a_log": _row(gdn_a_log[l], LANES),
        "gdn_dt_bias": _row(gdn_dt_bias[l], LANES), "gdn_onorm": jnp.tile(_row(gdn_onorm[l]), (1, H_GDN)),
        "w_out": [wo[0:256], wo[256:768], wo[768:1024]],
        "norm_ca": _row(norm_ca[l]), "ca_wq": ca_wq[l].astype(BF16), "ca_wo": ca_wo[l].astype(BF16),
        "ca_qnorm": _row(ca_qnorm[l]), "norm_ffn": _row(norm_ffn[l]),
        "ffn_w1": ffn_w1[l].astype(BF16), "ffn_w3": ffn_w3[l].astype(BF16), "ffn_w2": ffn_w2[l].astype(BF16),
    }


def _forward(x_prompt, x_sample, mem_prompt, cache_k, cache_v, cache_mem_k, cache_mem_v, state_gla, state_gdn,
             state_conv, page_table, norm_mix, w_in, gla_w_a2, gla_b_a, gla_onorm, diff_qnorm, diff_knorm,
             lam_q1, lam_k1, lam_q2, lam_k2, diff_onorm, gdn_conv_w, gdn_a_log, gdn_dt_bias, gdn_onorm, w_out,
             norm_ca, norm_mem, ca_wq, ca_wk, ca_wv, ca_wo, ca_qnorm, ca_knorm, norm_ffn, ffn_w1, ffn_w3, ffn_w2,
             *, chunk_p, tm_p, tq_p, rows_s, pages):
    depth = w_in.shape[0]
    bp, tp, _ = x_prompt.shape
    bs, ts, _ = x_sample.shape
    n_mem = mem_prompt.shape[1]
    slopes = 2.0 ** (-8.0 * jnp.arange(1, H_DIFF + 1, dtype=F32) / H_DIFF)
    consts_p = _consts(chunk_p)
    consts_s = _consts(rows_s)
    p64_512 = _block_ones(DIFF_W, DK_DIFF).astype(BF16)

    xp = x_prompt.reshape(bp * tp, D_MODEL)
    xs = jnp.pad(x_sample, ((0, 0), (0, rows_s - ts), (0, 0))).reshape(bs * rows_s, D_MODEL)
    memp = mem_prompt.reshape(bp * n_mem, D_MODEL)
    zeros_gla = jnp.zeros((bp, H_GLA, DK_GLA, DV_GLA), F32)
    zeros_gdn = jnp.zeros((bp, H_GDN, DK_GDN, DV_GDN), F32)
    zeros_conv = jnp.zeros((bp, CONV_W - 1, C_CONV), F32)
    tm_m = min(bp * n_mem, 512)
    tm_s = bs * rows_s
    cache_k = cache_k.reshape(cache_k.shape[0], cache_k.shape[1], PAGE_SIZE * H_DIFF, LANES)
    cache_v = cache_v.reshape(cache_v.shape[0], cache_v.shape[1], PAGE_SIZE * H_DIFF, LANES)

    outs = [[] for _ in range(12)]
    for l in range(depth):
        wts = _layer_weights(l, norm_mix, w_in, gla_w_a2, gla_b_a, gla_onorm, diff_qnorm, diff_knorm, lam_q1,
                             lam_k1, lam_q2, lam_k2, diff_onorm, gdn_conv_w, gdn_a_log, gdn_dt_bias, gdn_onorm,
                             w_out, norm_ca, ca_wq, ca_wo, ca_qnorm, norm_ffn, ffn_w1, ffn_w3, ffn_w2)
        mk = _norm_mm(memp, _row(norm_mem[l]), ca_wk[l].astype(BF16), _row(ca_knorm[l]), tm_m, headnorm=True,
                      scale=1.0, out_dtype=F32).reshape(1, bp, n_mem, H_CA, DH_CA)
        mv = _norm_mm(memp, _row(norm_mem[l]), ca_wv[l].astype(BF16), _row(ca_knorm[l]), tm_m, headnorm=False,
                      scale=1.0, out_dtype=F32).reshape(1, bp, n_mem, H_CA, DH_CA)

        def diff_p(q16, k16, v16, vt16, lamp, onorm, lam_init):
            return _diff_prompt(q16, k16, vt16, slopes, lamp, onorm, bp, tp, tq_p, lam_init)

        xp, kp, vp, sgp, sdp, cvp = _layer(xp, wts, l, bp, tp, chunk_p, chunk_p, tm_p, zeros_gla, zeros_gdn,
                                           zeros_conv, mk, mv, 0, diff_p, consts_p, p64_512)

        def diff_s(q16, k16, v16, vt16, lamp, onorm, lam_init, l=l):
            return _diff_decode(q16, k16, v16, cache_k, cache_v, page_table, slopes, lamp, onorm, l, bs, rows_s,
                                ts, lam_init, pages)

        xs, ks, vs, sgs, sds, cvs = _layer(xs, wts, l, bs, rows_s, rows_s, ts, tm_s, state_gla[l], state_gdn[l],
                                           state_conv[l], cache_mem_k, cache_mem_v, l, diff_s, consts_s, p64_512)
        vals = (kp.reshape(bp, tp, H_DIFF, 2 * DK_DIFF), vp.reshape(bp, tp, H_DIFF, DV_DIFF),
                ks.reshape(bs, rows_s, H_DIFF, 2 * DK_DIFF)[:, :ts], vs.reshape(bs, rows_s, H_DIFF, DV_DIFF)[:, :ts],
                sgp, sgs, sdp, sds, cvp, cvs, mk[0], mv[0])
        for o, v in zip(outs, vals):
            o.append(v)

    y_p = xp.reshape(bp, tp, D_MODEL)
    y_s = xs.reshape(bs, rows_s, D_MODEL)[:, :ts]
    return (y_p, y_s) + tuple(jnp.stack(o) for o in outs)


def kernel(x_prompt, x_sample, mem_prompt, cache_k, cache_v, cache_mem_k, cache_mem_v, state_gla, state_gdn, state_conv, page_table, norm_mix, w_in, gla_w_a2, gla_b_a, gla_onorm, diff_qnorm, diff_knorm, lam_q1, lam_k1, lam_q2, lam_k2, diff_onorm, gdn_conv_w, gdn_a_log, gdn_dt_bias, gdn_onorm, w_out, norm_ca, norm_mem, ca_wq, ca_wk, ca_wv, ca_wo, ca_qnorm, ca_knorm, norm_ffn, ffn_w1, ffn_w3, ffn_w2):
    return _forward(x_prompt, x_sample, mem_prompt, cache_k, cache_v, cache_mem_k, cache_mem_v, state_gla,
                    state_gdn, state_conv, page_table, norm_mix, w_in, gla_w_a2, gla_b_a, gla_onorm, diff_qnorm,
                    diff_knorm, lam_q1, lam_k1, lam_q2, lam_k2, diff_onorm, gdn_conv_w, gdn_a_log, gdn_dt_bias,
                    gdn_onorm, w_out, norm_ca, norm_mem, ca_wq, ca_wk, ca_wv, ca_wo, ca_qnorm, ca_knorm, norm_ffn,
                    ffn_w1, ffn_w3, ffn_w2, chunk_p=64, tm_p=512, tq_p=512, rows_s=16, pages=16)
```

```python
import functools
import math

import jax
import jax.numpy as jnp
from jax import lax
from jax.experimental import pallas as pl
from jax.experimental.pallas import tpu as pltpu

F32 = jnp.float32
BF16 = jnp.bfloat16

D_MODEL = 1024
DEPTH = 4
PAGE_SIZE = 128
H_GLA, DK_GLA, DV_GLA, GLA_RANK, GLA_TAU = 4, 32, 64, 16, 16.0
H_DIFF, DK_DIFF, DV_DIFF = 4, 64, 128
H_GDN, DK_GDN, DV_GDN, CONV_W = 4, 64, 64, 4
C_CONV = 2 * H_GDN * DK_GDN + H_GDN * DV_GDN
N_MEM, H_CA = 256, 4
DH_CA = D_MODEL // H_CA
D_FF = -(-8 * D_MODEL // (3 * 256)) * 256
RMS_EPS = 1e-6
L2_EPS = 1e-6

VMEM_LIMIT_V7X = 56 * 1024 * 1024
LANES = 128

GLA_W = 896
GDN_W = 1152
DIFF_W = 512
OFF_GLA, OFF_DQ, OFF_DK, OFF_DV, OFF_GDN = 0, 896, 1408, 1920, 2432
IN_W = OFF_GDN + GDN_W

NEG_BIG = -1e30


def _dot(a, b):
    return jnp.dot(a, b, preferred_element_type=F32)


def _dot_nt(a, b):
    return lax.dot_general(a, b, (((1,), (1,)), ((), ())), preferred_element_type=F32)


def _hilo(a):
    hi = a.astype(BF16)
    lo = (a - hi.astype(F32)).astype(BF16)
    return hi, lo


def _dot_sel_rhs(a, sel):
    hi, lo = _hilo(a)
    return _dot(hi, sel) + _dot(lo, sel)


def _dot_sel_lhs(sel, b):
    hi, lo = _hilo(b)
    return _dot(sel, hi) + _dot(sel, lo)


def _dot3(a, b):
    ah, al = _hilo(a)
    bh, bl = _hilo(b)
    return _dot(ah, bh) + _dot(ah, bl) + _dot(al, bh)


def _sigmoid(x):
    return 1.0 / (1.0 + jnp.exp(-x))


def _softplus(x):
    return jnp.maximum(x, 0.0) + jnp.log(1.0 + jnp.exp(-jnp.abs(x)))


def _rms_rows(x, g):
    ms = jnp.mean(x * x, axis=-1, keepdims=True)
    return x * lax.rsqrt(ms + RMS_EPS) * g


def _pad_rows_128(a):
    r = a.shape[0]
    if r == LANES:
        return a
    return jnp.concatenate([a, jnp.zeros((LANES - r, a.shape[1]), a.dtype)], axis=0)


def _const_spec(shape):
    nd = len(shape)
    return pl.BlockSpec(shape, lambda *_: (0,) * nd)


def _params(sem):
    return pltpu.CompilerParams(dimension_semantics=sem, vmem_limit_bytes=VMEM_LIMIT_V7X)


def _in_proj_kernel(x_ref, g_ref, w_ref, qkg_ref, p64_ref, *rest):
    gla_ref, q_ref, k_ref, kb_ref, v_ref, vb_ref, vt_ref, gdn_ref = rest[-8:]
    xn = _rms_rows(x_ref[...], g_ref[...]).astype(BF16)
    gla_ref[...] = _dot(xn, w_ref[:, OFF_GLA:OFF_GLA + GLA_W])

    def qknorm(y, gain):
        ss = _dot_sel_rhs(y * y, p64_ref[...])
        return y * lax.rsqrt(ss * (1.0 / DK_DIFF) + RMS_EPS) * gain

    yq = _dot(xn, w_ref[:, OFF_DQ:OFF_DQ + DIFF_W])
    q_ref[...] = qknorm(yq, qkg_ref[:, 0:DIFF_W]).astype(BF16)
    yk = _dot(xn, w_ref[:, OFF_DK:OFF_DK + DIFF_W])
    kk = qknorm(yk, qkg_ref[:, DIFF_W:2 * DIFF_W])
    k_ref[...] = kk
    kb_ref[...] = kk.astype(BF16)
    yv = _dot(xn, w_ref[:, OFF_DV:OFF_DV + DIFF_W])
    v_ref[...] = yv
    vb_ref[...] = yv.astype(BF16)
    vt_ref[...] = jnp.transpose(yv).astype(BF16)
    gdn_ref[...] = _dot(xn, w_ref[:, OFF_GDN:OFF_GDN + GDN_W])


def _in_proj(x, g, w, qkg, p64, tm, layer, depth, stacks):
    m = x.shape[0]
    row = lambda wd: pl.BlockSpec((tm, wd), lambda i: (i, 0))
    slot = pl.BlockSpec((None, tm, DIFF_W), lambda i: (layer, i, 0))
    stack_shape = jax.ShapeDtypeStruct((depth, m, DIFF_W), F32)
    in_specs = [row(D_MODEL), _const_spec((1, D_MODEL)), _const_spec((D_MODEL, IN_W)),
                _const_spec((1, 2 * DIFF_W)), _const_spec((DIFF_W, DIFF_W))]
    args = [x, g, w, qkg, p64]
    aliases = {}
    if stacks is not None:
        in_specs += [pl.BlockSpec(memory_space=pl.ANY), pl.BlockSpec(memory_space=pl.ANY)]
        args += list(stacks)
        aliases = {5: 2, 6: 4}
    return pl.pallas_call(
        _in_proj_kernel,
        grid=(m // tm,),
        in_specs=in_specs,
        out_specs=[row(GLA_W), row(DIFF_W), slot, row(DIFF_W), slot, row(DIFF_W),
                   pl.BlockSpec((DIFF_W, tm), lambda i: (0, i)), row(GDN_W)],
        out_shape=[jax.ShapeDtypeStruct((m, GLA_W), F32), jax.ShapeDtypeStruct((m, DIFF_W), BF16),
                   stack_shape, jax.ShapeDtypeStruct((m, DIFF_W), BF16),
                   stack_shape, jax.ShapeDtypeStruct((m, DIFF_W), BF16),
                   jax.ShapeDtypeStruct((DIFF_W, m), BF16), jax.ShapeDtypeStruct((m, GDN_W), F32)],
        input_output_aliases=aliases,
        compiler_params=_params(("parallel",)),
        name="in_proj",
    )(*args)


def _norm_mm_kernel(x_ref, g_ref, w_ref, hg_ref, o_ref, *maybe_heads_ref, headnorm, scale):
    xn = _rms_rows(x_ref[...], g_ref[...]).astype(BF16)
    y = _dot(xn, w_ref[...])
    for h in range(H_CA):
        yh = y[:, h * DH_CA:(h + 1) * DH_CA]
        if headnorm:
            yh = _rms_rows(yh, hg_ref[...]) * scale
        o_ref[:, h * DH_CA:(h + 1) * DH_CA] = yh.astype(o_ref.dtype)
        if maybe_heads_ref:
            maybe_heads_ref[0][h] = yh.astype(BF16)


def _norm_mm(x, g, w, hg, tm, *, headnorm, scale, out_dtype, head_major_copy=False):
    m, n = x.shape[0], w.shape[1]
    out_specs = [pl.BlockSpec((tm, n), lambda i: (i, 0))]
    out_shape = [jax.ShapeDtypeStruct((m, n), out_dtype)]
    if head_major_copy:
        out_specs.append(pl.BlockSpec((H_CA, tm, DH_CA), lambda i: (0, i, 0)))
        out_shape.append(jax.ShapeDtypeStruct((H_CA, m, DH_CA), BF16))
    outs = pl.pallas_call(
        functools.partial(_norm_mm_kernel, headnorm=headnorm, scale=scale),
        grid=(m // tm,),
        in_specs=[pl.BlockSpec((tm, D_MODEL), lambda i: (i, 0)), _const_spec((1, D_MODEL)),
                  _const_spec((D_MODEL, n)), _const_spec((1, DH_CA))],
        out_specs=out_specs,
        out_shape=out_shape,
        compiler_params=_params(("parallel",)),
        name="norm_mm",
    )(x, g, w, hg)
    return outs if head_major_copy else outs[0]


def _mm_res_kernel(*refs, n):
    xs, ws, res_ref, o_ref = refs[:n], refs[n:2 * n], refs[2 * n], refs[2 * n + 1]
    acc = res_ref[...]
    for x_ref, w_ref in zip(xs, ws):
        acc = acc + _dot(x_ref[...], w_ref[...])
    o_ref[...] = acc


def _mm_res(xs, ws, res, tm):
    m, n = res.shape
    k = len(xs)
    in_specs = ([pl.BlockSpec((tm, x.shape[1]), lambda i: (i, 0)) for x in xs]
                + [_const_spec(w.shape) for w in ws]
                + [pl.BlockSpec((tm, n), lambda i: (i, 0))])
    return pl.pallas_call(
        functools.partial(_mm_res_kernel, n=k),
        grid=(m // tm,),
        in_specs=in_specs,
        out_specs=pl.BlockSpec((tm, n), lambda i: (i, 0)),
        out_shape=jax.ShapeDtypeStruct((m, n), F32),
        compiler_params=_params(("parallel",)),
        name="mm_res",
    )(*xs, *ws, res)


FF_CHUNK = D_FF // 2


def _swiglu_kernel(x_ref, g_ref, w1_ref, w3_ref, o_ref):
    xn = _rms_rows(x_ref[...], g_ref[...]).astype(BF16)
    for c in range(D_FF // FF_CHUNK):
        sl = slice(c * FF_CHUNK, (c + 1) * FF_CHUNK)
        a = _dot(xn, w1_ref[:, sl])
        b = _dot(xn, w3_ref[:, sl])
        o_ref[:, sl] = (a * _sigmoid(a) * b).astype(BF16)


def _swiglu(x, g, w1, w3, tm):
    m = x.shape[0]
    return pl.pallas_call(
        _swiglu_kernel,
        grid=(m // tm,),
        in_specs=[pl.BlockSpec((tm, D_MODEL), lambda i: (i, 0)), _const_spec((1, D_MODEL)),
                  _const_spec((D_MODEL, D_FF)), _const_spec((D_MODEL, D_FF))],
        out_specs=pl.BlockSpec((tm, D_FF), lambda i: (i, 0)),
        out_shape=jax.ShapeDtypeStruct((m, D_FF), BF16),
        compiler_params=_params(("parallel",)),
        name="swiglu",
    )(x, g, w1, w3)


GLA_SB = 16


def _gla_chunk(p, s, wa_ref, ba_ref, on_ref, cum_ref, seg_ref, p64_ref, bd_ref, b_s, q_s, k_s, v_s, tall_s,
               *, chunk, n_valid):
    hk, hv = H_GLA * DK_GLA, H_GLA * DV_GLA
    nsb = chunk // GLA_SB
    q = p[:, 0:128] * (DK_GLA ** -0.5)
    k = p[:, 128:256]
    v = p[:, 256:512]
    gate = p[:, 512:768]
    x = _dot(p[:, 768:896].astype(BF16), wa_ref[...]) + ba_ref[...]
    la = (jnp.minimum(x, 0.0) - jnp.log(1.0 + jnp.exp(-jnp.abs(x)))) * (1.0 / GLA_TAU)
    if n_valid < chunk:
        rowv = lax.broadcasted_iota(jnp.int32, (chunk, 1), 0) < n_valid
        la = jnp.where(rowv, la, 0.0)
        k = jnp.where(rowv, k, 0.0)
    cums = _dot_sel_lhs(cum_ref[...], la)
    b = cums[0:chunk]
    b_s[...] = b
    q_s[...] = q
    k_s[...] = k
    v_s[...] = v
    v16 = v.astype(BF16)
    o = _dot((q * jnp.exp(b)).astype(BF16), s.astype(BF16))

    if nsb > 1:
        mq, mqc = cums[chunk:2 * chunk], cums[2 * chunk:3 * chunk]
        mh, mhc = cums[3 * chunk:4 * chunk], cums[4 * chunk:5 * chunk]
        lane = lax.broadcasted_iota(jnp.int32, (1, hk), 1)

        def stacked(a):
            return jnp.concatenate(
                [jnp.where((lane >= h * DK_GLA) & (lane < (h + 1) * DK_GLA), a, 0.0) for h in range(H_GLA)],
                axis=0).astype(BF16)

        a2 = _dot_nt(stacked(q * jnp.exp(b - mq)), (k * jnp.exp(mqc - b)).astype(BF16))
        a1 = _dot_nt(stacked(q * jnp.exp(b - mh)), (k * jnp.exp(mhc - b)).astype(BF16))
        qi = (lax.broadcasted_iota(jnp.int32, (H_GLA * chunk, chunk), 0) & (chunk - 1)) // GLA_SB
        qj = lax.broadcasted_iota(jnp.int32, (H_GLA * chunk, chunk), 1) // GLA_SB
        att = jnp.where(qi == qj + 1, a2, jnp.where(qi > qj + 1, a1, 0.0))
        r = _dot(att.astype(BF16), v16)
        lane_v = lax.broadcasted_iota(jnp.int32, (1, hv), 1)
        for h in range(H_GLA):
            o = o + jnp.where((lane_v >= h * DV_GLA) & (lane_v < (h + 1) * DV_GLA),
                              r[h * chunk:(h + 1) * chunk], 0.0)

    row = lax.broadcasted_iota(jnp.int32, (GLA_SB, 1), 0)
    for j in range(GLA_SB):
        for d in range(nsb):
            sl = slice(d * GLA_SB, (d + 1) * GLA_SB)
            bj = b_s[d * GLA_SB + j:d * GLA_SB + j + 1, :]
            kj = k_s[d * GLA_SB + j:d * GLA_SB + j + 1, :]
            e = jnp.exp(jnp.where(row >= j, b_s[sl, :] - bj, -jnp.inf))
            tall_s[j * chunk + d * GLA_SB:j * chunk + (d + 1) * GLA_SB, :] = (q_s[sl, :] * kj * e).astype(BF16)
    tsum = _dot(tall_s[...], seg_ref[...])
    for j in range(GLA_SB):
        vj = jnp.concatenate(
            [jnp.broadcast_to(v_s[d * GLA_SB + j:d * GLA_SB + j + 1, :], (GLA_SB, hv)) for d in range(nsb)], axis=0)
        o = o + tsum[j * chunk:(j + 1) * chunk] * vj

    b_end = b[chunk - 1:chunk, :]
    kd = k * jnp.exp(b_end - b)
    kd_t = jnp.transpose(_pad_rows_128(kd)).astype(BF16)
    upd = _dot(kd_t, _pad_rows_128(v).astype(BF16))
    dcol = jnp.transpose(jnp.broadcast_to(jnp.exp(b_end), (LANES, LANES)))[:, 0:1]
    s_new = s * dcol + upd * bd_ref[...]

    ss = _dot_sel_rhs(o * o, p64_ref[...])
    on = o * lax.rsqrt(ss * (1.0 / DV_GLA) + RMS_EPS) * on_ref[...]
    return on * gate * _sigmoid(gate), s_new


def _gla_kernel(p_ref, s0_ref, wa_ref, ba_ref, on_ref, cum_ref, seg_ref, p64_ref, bd_ref,
                o_ref, sout_ref, s_s, b_s, q_s, k_s, v_s, tall_s, *, chunk, n_chunks, n_valid):
    c = pl.program_id(1)

    @pl.when(c == 0)
    def _():
        s_s[...] = s0_ref[0]

    step = functools.partial(_gla_chunk, wa_ref=wa_ref, ba_ref=ba_ref, on_ref=on_ref, cum_ref=cum_ref,
                             seg_ref=seg_ref, p64_ref=p64_ref, bd_ref=bd_ref, b_s=b_s, q_s=q_s, k_s=k_s, v_s=v_s,
                             tall_s=tall_s, chunk=chunk, n_valid=n_valid)

    def body(ci, s):
        r0 = pl.multiple_of(ci * chunk, chunk)
        o, s_new = step(p_ref[pl.ds(r0, chunk), :], s)
        o_ref[pl.ds(r0, chunk), :] = o.astype(BF16)
        return s_new

    s_new = lax.fori_loop(0, n_chunks, body, s_s[...])
    s_s[...] = s_new

    @pl.when(c == pl.num_programs(1) - 1)
    def _():
        sout_ref[0] = s_new


def _gla(proj, s0, wa, ba, onorm, consts, bc, t, chunk, n_valid, block):
    nb = t // block
    hk, hv = H_GLA * DK_GLA, H_GLA * DV_GLA
    cum = consts["gla_cum"]
    return pl.pallas_call(
        functools.partial(_gla_kernel, chunk=chunk, n_chunks=block // chunk, n_valid=n_valid),
        grid=(bc, nb),
        in_specs=[pl.BlockSpec((block, GLA_W), lambda b, c: (b * nb + c, 0)),
                  pl.BlockSpec((1, hk, hv), lambda b, c: (b, 0, 0)),
                  _const_spec((LANES, hk)), _const_spec((1, hk)), _const_spec((1, hv)),
                  _const_spec(cum.shape), _const_spec((hk, hv)), _const_spec((hv, hv)),
                  _const_spec((hk, hv))],
        out_specs=[pl.BlockSpec((block, hv), lambda b, c: (b * nb + c, 0)),
                   pl.BlockSpec((1, hk, hv), lambda b, c: (b, 0, 0))],
        out_shape=[jax.ShapeDtypeStruct((bc * t, hv), BF16), jax.ShapeDtypeStruct((bc, hk, hv), F32)],
        scratch_shapes=[pltpu.VMEM((hk, hv), F32), pltpu.VMEM((chunk, hk), F32), pltpu.VMEM((chunk, hk), F32),
                        pltpu.VMEM((chunk, hk), F32), pltpu.VMEM((chunk, hv), F32),
                        pltpu.VMEM((GLA_SB * chunk, hk), BF16)],
        compiler_params=_params(("parallel", "arbitrary")),
        name="gla",
    )(proj, s0, wa, ba, onorm, cum, consts["seg_gla"], consts["p64_256"], consts["bd_gla"])


def _gdn_kernel(p_ref, s0_ref, cb_ref, cw_ref, alog_ref, dtb_ref, on_ref,
                tri_ref, p64_ref, eg_ref, eg64_ref, eb64_ref, bd_ref,
                o_ref, sout_ref, cout_ref, s_s, xp_s, *, chunk, n_valid, nseq):
    c = pl.program_id(1)
    hd = H_GDN * DK_GDN
    last = c == pl.num_programs(1) - 1

    @pl.when(c == 0)
    def _():
        s_s[...] = s0_ref[...]
        xp_s[:, 5:8, :] = cb_ref[...]

    def l2n(a):
        ss = _dot_sel_rhs(a * a, p64_ref[...])
        return a * lax.rsqrt(ss + L2_EPS)

    ri = lax.broadcasted_iota(jnp.int32, (chunk, chunk), 0)
    ci = lax.broadcasted_iota(jnp.int32, (chunk, chunk), 1)
    eye = (ri == ci).astype(F32)
    lane = lax.broadcasted_iota(jnp.int32, (1, hd), 1)
    lms = [(lane >= h * DK_GDN) & (lane < (h + 1) * DK_GDN) for h in range(H_GDN)]
    n_sq = int(math.log2(chunk)) - 1

    pre = []
    for n in range(nseq):
        p = p_ref[n]
        xp_s[n, 8:8 + chunk, :] = p[:, 0:C_CONV]
        y = xp_s[n, pl.ds(5, chunk), :] * cw_ref[0:1, :]
        for i in range(1, CONV_W):
            y = y + xp_s[n, pl.ds(5 + i, chunk), :] * cw_ref[i:i + 1, :]
        y = y * _sigmoid(y)
        tail = xp_s[n, pl.ds(5 + n_valid, CONV_W - 1), :]
        xp_s[n, 5:8, :] = tail

        @pl.when(last)
        def _(n=n, tail=tail):
            cout_ref[n] = tail

        cq = l2n(y[:, 0:hd]) * (DK_GDN ** -0.5)
        ck = l2n(y[:, hd:2 * hd])
        cv = y[:, 2 * hd:3 * hd]
        small = p[:, 1024:1152]
        g = -jnp.exp(alog_ref[...]) * _softplus(small + dtb_ref[...])
        beta = _sigmoid(small)
        if n_valid < chunk:
            rowv = lax.broadcasted_iota(jnp.int32, (chunk, 1), 0) < n_valid
            g = jnp.where(rowv, g, 0.0)
            beta = jnp.where(rowv, beta, 0.0)
        gam = _dot_sel_lhs(tri_ref[...], g)
        gam_w = _dot_sel_rhs(gam, eg_ref[...])
        gam_b = _dot_sel_rhs(gam, eg64_ref[...])
        beta_b = _dot_sel_rhs(beta, eb64_ref[...])
        gam_t = jnp.transpose(_pad_rows_128(gam))
        eg = jnp.exp(gam_b)
        kb = ck * beta_b
        g_end = gam_b[chunk - 1:chunk, :]
        s = s_s[n]
        pre.append(dict(
            z=p[:, 768:1024], cq=cq, ck16=ck.astype(BF16), eg=eg, kb=kb, kbg=kb * eg, vb=cv * beta_b, g_end=g_end,
            kdec=ck * jnp.exp(g_end - gam_b), s=s, s16=s.astype(BF16),
            decs=[jnp.exp(jnp.where(ri >= ci, gam_w[:, h * LANES:h * LANES + chunk] - gam_t[h:h + 1, 0:chunk],
                                    -jnp.inf)) for h in range(H_GDN)]))

    pairs = [(n, h) for n in range(nseq) for h in range(H_GDN)]
    pws = [-(_dot_nt(jnp.where(lms[h], pre[n]["kb"], 0.0).astype(BF16), pre[n]["ck16"])
             * jnp.where(ri > ci, pre[n]["decs"][h], 0.0)) for n, h in pairs]
    atts = [(_dot_nt(jnp.where(lms[h], pre[n]["cq"], 0.0).astype(BF16), pre[n]["ck16"])
             * pre[n]["decs"][h]).astype(BF16) for n, h in pairs]
    tinvs = [eye + pw for pw in pws]
    for _ in range(n_sq):
        pws = [_dot3(pw, pw) for pw in pws]
        tinvs = [ti + _dot3(ti, pw) for ti, pw in zip(tinvs, pws)]
    sols = [_dot3(ti, jnp.concatenate([jnp.where(lms[h], pre[n]["vb"], 0.0), jnp.where(lms[h], pre[n]["kbg"], 0.0)],
                                      axis=1)) for ti, (n, h) in zip(tinvs, pairs)]

    def seq_sum(n, lo):
        parts = [sols[n * H_GDN + h][:, lo:lo + hd] for h in range(H_GDN)]
        return (parts[0] + parts[1]) + (parts[2] + parts[3])

    v_news = [seq_sum(n, 0) - _dot(seq_sum(n, hd).astype(BF16), pre[n]["s16"]) for n in range(nseq)]
    outs = [_dot((pre[n]["cq"] * pre[n]["eg"]).astype(BF16), pre[n]["s16"]) for n in range(nseq)]
    intra = [_dot(atts[i], jnp.where(lms[h], v_news[n], 0.0).astype(BF16)) for i, (n, h) in enumerate(pairs)]
    upds = [_dot(jnp.transpose(_pad_rows_128(pre[n]["kdec"])).astype(BF16), _pad_rows_128(v_news[n]).astype(BF16))
            for n in range(nseq)]
    for n in range(nseq):
        o = outs[n]
        for h in range(H_GDN):
            o = o + intra[n * H_GDN + h]
        s_new = pre[n]["s"] * jnp.exp(pre[n]["g_end"]) + upds[n] * bd_ref[...]
        s_s[n] = s_new
        ss = _dot_sel_rhs(o * o, p64_ref[...])
        on = o * lax.rsqrt(ss * (1.0 / DV_GDN) + RMS_EPS) * on_ref[...]
        z = pre[n]["z"]
        o_ref[n] = (on * z * _sigmoid(z)).astype(BF16)

        @pl.when(last)
        def _(n=n, s_new=s_new):
            sout_ref[n] = s_new


def _gdn(proj, s0, conv0, cw, alog, dtb, onorm, consts, bc, t, chunk, n_valid, nseq):
    nc = t // chunk
    hd = H_GDN * DK_GDN
    assert bc % nseq == 0
    o, s_out, conv_out = pl.pallas_call(
        functools.partial(_gdn_kernel, chunk=chunk, n_valid=n_valid, nseq=nseq),
        grid=(bc // nseq, nc),
        in_specs=[pl.BlockSpec((nseq, chunk, GDN_W), lambda b, c: (b, c, 0)),
                  pl.BlockSpec((nseq, hd, hd), lambda b, c: (b, 0, 0)),
                  pl.BlockSpec((nseq, CONV_W - 1, C_CONV), lambda b, c: (b, 0, 0)),
                  _const_spec((CONV_W, C_CONV)), _const_spec((1, LANES)), _const_spec((1, LANES)),
                  _const_spec((1, hd)),
                  _const_spec((chunk, chunk)), _const_spec((hd, hd)), _const_spec((LANES, 4 * LANES)),
                  _const_spec((LANES, hd)), _const_spec((LANES, hd)), _const_spec((hd, hd))],
        out_specs=[pl.BlockSpec((nseq, chunk, hd), lambda b, c: (b, c, 0)),
                   pl.BlockSpec((nseq, hd, hd), lambda b, c: (b, 0, 0)),
                   pl.BlockSpec((nseq, CONV_W - 1, C_CONV), lambda b, c: (b, 0, 0))],
        out_shape=[jax.ShapeDtypeStruct((bc, t, hd), BF16), jax.ShapeDtypeStruct((bc, hd, hd), F32),
                   jax.ShapeDtypeStruct((bc, CONV_W - 1, C_CONV), F32)],
        scratch_shapes=[pltpu.VMEM((nseq, hd, hd), F32), pltpu.VMEM((nseq, chunk + 8, C_CONV), F32)],
        compiler_params=_params(("parallel", "arbitrary")),
        name="gdn",
    )(proj.reshape(bc, t, GDN_W), s0, conv0, cw, alog, dtb, onorm, consts["tri"], consts["p64_256"], consts["eg"],
      consts["eg64"], consts["eb64"], consts["bd_gdn"])
    return o.reshape(bc * t, hd), s_out, conv_out


def _lam_value(lamp_ref, lam_init):
    lp = lamp_ref[...]
    return (jnp.exp(jnp.sum(lp[0:1] * lp[1:2], axis=-1, keepdims=True))
            - jnp.exp(jnp.sum(lp[2:3] * lp[3:4], axis=-1, keepdims=True)) + lam_init)


def _softmax_step(carry, s, v16):
    m, l, acc = carry
    m_new = jnp.maximum(m, jnp.max(s, axis=-1, keepdims=True))
    alpha = jnp.exp(m - m_new)
    p = jnp.exp(s - m_new)
    l = alpha * l + jnp.sum(p, axis=-1, keepdims=True)
    acc = alpha * acc + _dot(p.astype(BF16), v16)
    return m_new, l, acc


def _diff_prompt_kernel(slope_ref, lamp_ref, ong_ref, q_ref, k_ref, vt_ref, o_ref, *, tq, lam_init):
    h = pl.program_id(1)
    qi = pl.program_id(2)
    w = 2 * tq
    slope = slope_ref[h]
    lam = _lam_value(lamp_ref, lam_init)
    krow = lax.broadcasted_iota(jnp.int32, (tq, w), 0)
    base = slope * krow.astype(F32)
    rowc = lax.broadcasted_iota(jnp.int32, (LANES, tq), 0)
    q_t = jnp.transpose(q_ref[...].astype(F32))
    qt = jnp.concatenate([jnp.where(rowc < DK_DIFF, q_t, 0.0), jnp.where(rowc >= DK_DIFF, q_t, 0.0)],
                         axis=1).astype(BF16)

    def block(kj, carry, masked):
        m, l, acc = carry
        off = pl.multiple_of(kj * tq, tq)
        s = _dot(k_ref[pl.ds(off, tq), :], qt) + base
        if masked:
            qpos = lax.broadcasted_iota(jnp.int32, (tq, w), 1) & (tq - 1)
            s = jnp.where(krow <= qpos, s, NEG_BIG)
        c = slope * ((kj - qi) * tq).astype(F32)
        m_new = jnp.maximum(m, jnp.max(s, axis=0, keepdims=True) + c)
        p = jnp.exp(s - (m_new - c))
        alpha = jnp.exp(m - m_new)
        l = alpha * l + jnp.sum(p, axis=0, keepdims=True)
        acc = alpha * acc + _dot(vt_ref[:, pl.ds(off, tq)], p.astype(BF16))
        return m_new, l, acc

    carry = (jnp.full((1, w), NEG_BIG, F32), jnp.zeros((1, w), F32), jnp.zeros((LANES, w), F32))
    carry = lax.fori_loop(0, qi, functools.partial(block, masked=False), carry)
    _, l, acc = block(qi, carry, True)
    o = acc / l
    d = o[:, 0:tq] - lam * o[:, tq:w]
    ms = jnp.mean(d * d, axis=0, keepdims=True)
    dn = d * lax.rsqrt(ms + RMS_EPS) * ong_ref[...] * (1.0 - lam_init)
    o_ref[...] = jnp.transpose(dn).astype(BF16)


def _diff_prompt(q, k16, vt16, slopes, lamp, onorm, bc, t, tq, lam_init):
    nq = t // tq
    assert tq & (tq - 1) == 0 and DV_DIFF == LANES
    ong = jnp.broadcast_to(onorm.reshape(DV_DIFF, 1), (DV_DIFF, tq))
    return pl.pallas_call(
        functools.partial(_diff_prompt_kernel, tq=tq, lam_init=lam_init),
        grid=(bc, H_DIFF, nq),
        in_specs=[pl.BlockSpec(memory_space=pltpu.SMEM), _const_spec((4, DK_DIFF)), _const_spec((DV_DIFF, tq)),
                  pl.BlockSpec((tq, LANES), lambda b, h, i: (b * nq + i, h)),
                  pl.BlockSpec((t, LANES), lambda b, h, i: (b, h)),
                  pl.BlockSpec((LANES, t), lambda b, h, i: (h, b))],
        out_specs=pl.BlockSpec((tq, LANES), lambda b, h, i: (b * nq + i, h)),
        out_shape=jax.ShapeDtypeStruct((bc * t, H_DIFF * DV_DIFF), BF16),
        compiler_params=_params(("parallel", "parallel", "arbitrary")),
        name="diff_prompt",
    )(slopes, lamp, ong, q, k16, vt16)


DEC_GROUP = 4


def _diff_decode_kernel(pt_ref, lamp_ref, on_ref, scol_ref, srow_ref, q_ref, kn_ref, vn_ref, *rest,
                        pages, rows, n_valid, past_len, lam_init):
    k_refs, v_refs = rest[:pages], rest[pages:2 * pages]
    o_ref, qq_s, m_s, l_s, acc_s = rest[2 * pages:]
    del pt_ref
    g = pl.program_id(1)
    qr = 2 * n_valid
    nr = H_DIFF * qr
    prow = PAGE_SIZE * H_DIFF
    rowh = lax.broadcasted_iota(jnp.int32, (nr, 1), 0) // qr

    @pl.when(g == 0)
    def _():
        row = lax.broadcasted_iota(jnp.int32, (rows, 1), 0)
        lane = lax.broadcasted_iota(jnp.int32, (rows, LANES), 1)
        for h in range(H_DIFF):
            q = q_ref[:, h * LANES:(h + 1) * LANES].astype(F32)
            q1 = jnp.where(lane < DK_DIFF, q, 0.0)
            q2 = pltpu.roll(jnp.where(lane >= DK_DIFF, q, 0.0), n_valid, axis=0)
            qq_s[h * qr:(h + 1) * qr, :] = jnp.where(row < n_valid, q1, q2)[0:qr]
        m_s[...] = jnp.full(m_s.shape, NEG_BIG, F32)
        l_s[...] = jnp.zeros(l_s.shape, F32)
        acc_s[...] = jnp.zeros(acc_s.shape, F32)

    qq = qq_s[...].astype(BF16)
    col = lax.broadcasted_iota(jnp.int32, (1, prow), 1)
    tcol = (col // H_DIFF).astype(F32)
    own = rowh == (col & (H_DIFF - 1))
    scol = scol_ref[...]
    groups = range(0, pages, DEC_GROUP)
    s_g = []
    for i0 in groups:
        ss = []
        for i in range(i0, i0 + DEC_GROUP):
            s = _dot_nt(qq, k_refs[i][...].astype(BF16))
            base = ((g * pages + i) * PAGE_SIZE - past_len).astype(F32)
            ss.append(jnp.where(own, s + scol * (tcol + base), NEG_BIG))
        s_g.append(jnp.concatenate(ss, axis=1))
    m_g = [jnp.max(s, axis=-1, keepdims=True) for s in s_g]
    p_g = [jnp.exp(s - m) for s, m in zip(s_g, m_g)]
    l_g = [jnp.sum(p, axis=-1, keepdims=True) for p in p_g]
    pv_g = []
    for i0, p32 in zip(groups, p_g):
        p = p32.astype(BF16)
        pv = _dot(p[:, 0:prow], v_refs[i0][...].astype(BF16))
        for i in range(1, DEC_GROUP):
            pv = pv + _dot(p[:, i * prow:(i + 1) * prow], v_refs[i0 + i][...].astype(BF16))
        pv_g.append(pv)
    m_prev = m_s[:, 0:1]
    m_new = m_prev
    for m in m_g:
        m_new = jnp.maximum(m_new, m)
    alpha = jnp.exp(m_prev - m_new)
    l_new = alpha * l_s[:, 0:1]
    acc_new = alpha * acc_s[...]
    for m, l, pv in zip(m_g, l_g, pv_g):
        a = jnp.exp(m - m_new)
        l_new = l_new + a * l
        acc_new = acc_new + a * pv
    m_s[...] = jnp.broadcast_to(m_new, m_s.shape)
    l_s[...] = jnp.broadcast_to(l_new, l_s.shape)
    acc_s[...] = acc_new

    @pl.when(g == pl.num_programs(1) - 1)
    def _():
        coln = lax.broadcasted_iota(jnp.int32, (1, rows), 1)
        tok = lax.broadcasted_iota(jnp.int32, (nr, 1), 0) & (n_valid - 1)
        s = jnp.zeros((nr, rows), F32)
        for h in range(H_DIFF):
            s = jnp.where(rowh == h, _dot_nt(qq, kn_ref[:, h * LANES:(h + 1) * LANES]), s)
        s = s + srow_ref[...] * coln.astype(F32)
        s = jnp.where((coln < n_valid) & (coln <= tok), s, NEG_BIG)
        m_fin = jnp.maximum(m_new, jnp.max(s, axis=-1, keepdims=True))
        alpha = jnp.exp(m_new - m_fin)
        p32 = jnp.exp(s - m_fin)
        l_fin = alpha * l_new + jnp.sum(p32, axis=-1, keepdims=True)
        p = p32.astype(BF16)
        acc = alpha * acc_new
        for h in range(H_DIFF):
            acc = acc + jnp.where(rowh == h, _dot(p, vn_ref[:, h * LANES:(h + 1) * LANES]), 0.0)
        o = acc / l_fin
        d = o - _lam_value(lamp_ref, lam_init) * pltpu.roll(o, nr - n_valid, axis=0)
        d = _rms_rows(d, on_ref[...]) * (1.0 - lam_init)
        row8 = lax.broadcasted_iota(jnp.int32, (qr, 1), 0)
        for h in range(H_DIFF):
            blk = jnp.where(row8 < n_valid, d[h * qr:(h + 1) * qr], 0.0)
            o_ref[:, h * LANES:(h + 1) * LANES] = jnp.concatenate(
                [blk, jnp.zeros((rows - qr, LANES), F32)], axis=0).astype(BF16)


def _diff_decode(q, kn16, vn16, cache_k, cache_v, page_table, slopes, lamp, onorm, layer, bc, rows, n_valid,
                 lam_init, pages):
    n_pages = page_table.shape[1]
    npg = n_pages // pages
    past_len = n_pages * PAGE_SIZE
    width = H_DIFF * LANES
    prow = PAGE_SIZE * H_DIFF
    nr = H_DIFF * 2 * n_valid
    assert pages % DEC_GROUP == 0 and n_valid & (n_valid - 1) == 0 and 2 * n_valid <= rows
    scol = jnp.tile(slopes, PAGE_SIZE).reshape(1, prow)
    srow = jnp.repeat(slopes, 2 * n_valid).reshape(nr, 1)

    def page_spec(i):
        return pl.BlockSpec((None, None, prow, LANES), lambda b, g, pt: (layer, pt[b, g * pages + i], 0, 0))

    tile = pl.BlockSpec((rows, width), lambda b, g, pt: (b, 0))
    grid_spec = pltpu.PrefetchScalarGridSpec(
        num_scalar_prefetch=1,
        grid=(bc, npg),
        in_specs=([pl.BlockSpec((4, DK_DIFF), lambda b, g, pt: (0, 0)),
                   pl.BlockSpec((1, DV_DIFF), lambda b, g, pt: (0, 0)),
                   pl.BlockSpec((1, prow), lambda b, g, pt: (0, 0)),
                   pl.BlockSpec((nr, 1), lambda b, g, pt: (0, 0)),
                   tile, tile, tile]
                  + [page_spec(i) for i in range(pages)] + [page_spec(i) for i in range(pages)]),
        out_specs=tile,
        scratch_shapes=[pltpu.VMEM((nr, LANES), F32), pltpu.VMEM((nr, LANES), F32),
                        pltpu.VMEM((nr, LANES), F32), pltpu.VMEM((nr, LANES), F32)],
    )
    return pl.pallas_call(
        functools.partial(_diff_decode_kernel, pages=pages, rows=rows, n_valid=n_valid, past_len=past_len,
                          lam_init=lam_init),
        grid_spec=grid_spec,
        out_shape=jax.ShapeDtypeStruct((bc * rows, width), BF16),
        compiler_params=_params(("parallel", "arbitrary")),
        name="diff_decode",
    )(page_table, lamp, onorm, scol, srow, q, kn16, vn16, *([cache_k] * pages), *([cache_v] * pages))


def _ca_kernel(q_ref, k_ref, v_ref, o_ref, *, interleaved):
    tq = q_ref.shape[0]
    if interleaved:
        colh = lax.broadcasted_iota(jnp.int32, (1, N_MEM * H_CA), 1) & (H_CA - 1)
        rowh = lax.broadcasted_iota(jnp.int32, (H_CA * tq, 1), 0) // tq
        qs = jnp.concatenate([q_ref[:, h * DH_CA:(h + 1) * DH_CA] for h in range(H_CA)], axis=0)
        s = jnp.where(rowh == colh, _dot_nt(qs, k_ref[...].astype(BF16)), NEG_BIG)
        p = jnp.exp(s - jnp.max(s, axis=-1, keepdims=True))
        a = p / jnp.sum(p, axis=-1, keepdims=True)
        o = _dot(a.astype(BF16), v_ref[...].astype(BF16))
        for h in range(H_CA):
            o_ref[:, h * DH_CA:(h + 1) * DH_CA] = o[h * tq:(h + 1) * tq].astype(BF16)
        return
    for h in range(H_CA):
        hs = slice(h * DH_CA, (h + 1) * DH_CA)
        s = _dot_nt(q_ref[:, hs], k_ref[h])
        p = jnp.exp(s - jnp.max(s, axis=-1, keepdims=True))
        a = p / jnp.sum(p, axis=-1, keepdims=True)
        o_ref[:, hs] = _dot(a.astype(BF16), v_ref[h]).astype(BF16)


def _ca_attn(q, mem_k, mem_v, bc, t, tq, mem_spec, interleaved):
    nq = t // tq
    return pl.pallas_call(
        functools.partial(_ca_kernel, interleaved=interleaved),
        grid=(bc, nq),
        in_specs=[pl.BlockSpec((tq, D_MODEL), lambda b, i: (b * nq + i, 0)), mem_spec, mem_spec],
        out_specs=pl.BlockSpec((tq, D_MODEL), lambda b, i: (b * nq + i, 0)),
        out_shape=jax.ShapeDtypeStruct((bc * t, D_MODEL), BF16),
        compiler_params=_params(("parallel", "arbitrary")),
        name="ca_attn",
    )(q, mem_k, mem_v)


def _block_ones(n, blk):
    i = jnp.arange(n) // blk
    return (i[:, None] == i[None, :])


def _consts(chunk):
    hk, hv = H_GLA * DK_GLA, H_GLA * DV_GLA
    r = jnp.arange(LANES)
    c512 = jnp.arange(4 * LANES)
    c256 = jnp.arange(256)
    i = jnp.arange(chunk)[:, None]
    t = jnp.arange(chunk)[None, :]
    cums = [t <= i]
    if chunk > GLA_SB:
        assert chunk == 4 * GLA_SB
        half = chunk // 2
        cums += [t < GLA_SB * (i // GLA_SB), t <= GLA_SB * (i // GLA_SB) + GLA_SB - 1,
                 t < half * (i // half), t <= half * (i // half) + half - 1]
    return {
        "gla_cum": jnp.concatenate(cums, axis=0).astype(BF16),
        "tri": jnp.tril(jnp.ones((chunk, chunk), F32)).astype(BF16),
        "seg_gla": ((jnp.arange(hk) // DK_GLA)[:, None] == (jnp.arange(hv) // DV_GLA)[None, :]).astype(BF16),
        "bd_gla": ((jnp.arange(hk) // DK_GLA)[:, None] == (jnp.arange(hv) // DV_GLA)[None, :]).astype(F32),
        "p64_256": _block_ones(256, 64).astype(BF16),
        "bd_gdn": _block_ones(256, 64).astype(F32),
        "eg": (r[:, None] == (c512 // LANES)[None, :]).astype(BF16),
        "eg64": (r[:, None] == (c256 // 64)[None, :]).astype(BF16),
        "eb64": (r[:, None] == (c256 // 64 + H_GDN)[None, :]).astype(BF16),
    }


def _expand_state(s, bd):
    b, h, dk, dv = s.shape
    return jnp.tile(s.reshape(b, h * dk, dv), (1, 1, h)) * bd


def _compact_state(s, h):
    b, hk, hv = s.shape
    s5 = s.reshape(b, h, hk // h, h, hv // h)
    return jnp.stack([s5[:, i, :, i, :] for i in range(h)], axis=1)


def _layer(x, wts, layer, depth, kv_stacks, bc, t, chunk, n_valid, tm, s_gla, s_gdn, conv0, ca_fn, diff_fn,
           consts, p64_512):
    lam_init = 0.8 - 0.6 * math.exp(-0.3 * layer)
    gla, q16, k_stack, k16, v_stack, v16, vt16, gdn = _in_proj(
        x, wts["norm_mix"], wts["w_in"], wts["qk_gain"], p64_512, tm, layer, depth, kv_stacks)
    o_gla, s_gla_new = _gla(gla, _expand_state(s_gla, consts["bd_gla"]), wts["gla_w_a2"], wts["gla_b_a"],
                            wts["gla_onorm"], consts, bc, t, chunk, n_valid, min(t, 256))
    o_diff = diff_fn(q16, k16, v16, vt16, wts["lamp"], wts["diff_onorm"], lam_init)
    o_gdn, s_gdn_new, conv_new = _gdn(gdn, _expand_state(s_gdn, consts["bd_gdn"]), conv0, wts["gdn_conv_w"],
                                      wts["gdn_a_log"], wts["gdn_dt_bias"], wts["gdn_onorm"], consts, bc, t,
                                      chunk, n_valid, 2 if bc % 2 == 0 else 1)
    x = _mm_res([o_gla, o_diff, o_gdn], wts["w_out"], x, tm)
    qc = _norm_mm(x, wts["norm_ca"], wts["ca_wq"], wts["ca_qnorm"], tm, headnorm=True, scale=DH_CA ** -0.5,
                  out_dtype=BF16)
    oc = ca_fn(qc)
    x = _mm_res([oc], [wts["ca_wo"]], x, tm)
    hid = _swiglu(x, wts["norm_ffn"], wts["ffn_w1"], wts["ffn_w3"], tm)
    x = _mm_res([hid], [wts["ffn_w2"]], x, tm)
    return (x, (k_stack, v_stack), _compact_state(s_gla_new, H_GLA), _compact_state(s_gdn_new, H_GDN), conv_new)


def _pack_w_in(w):
    sizes = (128, 128, 256, 16, 256, 512, 512, 512, 256, 256, 256, 256, 4, 4)
    segs, off = [], 0
    for s in sizes:
        segs.append(w[:, off:off + s])
        off += s
    gq, gk, gv, ga, gg, dq, dk, dv, nq, nk, nv, nz, na, nb = segs
    z = lambda n: jnp.zeros((w.shape[0], n), w.dtype)
    packed = jnp.concatenate([gq, gk, gv, gg, ga, z(GLA_W - 784), dq, dk, dv, nq, nk, nv, nz, na, nb,
                              z(GDN_W - 1032)], axis=1)
    return packed.astype(BF16)


def _row(v, n=None):
    v = v.reshape(1, -1).astype(F32)
    if n is not None and v.shape[1] < n:
        v = jnp.concatenate([v, jnp.zeros((1, n - v.shape[1]), F32)], axis=1)
    return v


def _layer_weights(l, norm_mix, w_in, gla_w_a2, gla_b_a, gla_onorm, diff_qnorm, diff_knorm, lam_q1, lam_k1,
                   lam_q2, lam_k2, diff_onorm, gdn_conv_w, gdn_a_log, gdn_dt_bias, gdn_onorm, w_out, norm_ca,
                   ca_wq, ca_wo, ca_qnorm, norm_ffn, ffn_w1, ffn_w3, ffn_w2):
    wo = w_out[l].astype(BF16)
    a2 = jnp.concatenate([gla_w_a2[l], jnp.zeros((LANES - GLA_RANK, H_GLA * DK_GLA), F32)], axis=0).astype(BF16)
    n_q = 2 * H_DIFF
    return {
        "norm_mix": _row(norm_mix[l]), "w_in": _pack_w_in(w_in[l]),
        "qk_gain": jnp.concatenate([jnp.tile(_row(diff_qnorm[l]), (1, n_q)) * (DK_DIFF ** -0.5),
                                    jnp.tile(_row(diff_knorm[l]), (1, n_q))], axis=1),
        "gla_w_a2": a2, "gla_b_a": _row(gla_b_a[l]), "gla_onorm": jnp.tile(_row(gla_onorm[l]), (1, H_GLA)),
        "lamp": jnp.stack([lam_q1[l], lam_k1[l], lam_q2[l], lam_k2[l]]).astype(F32),
        "diff_onorm": _row(diff_onorm[l]),
        "gdn_conv_w": gdn_conv_w[l].astype(F32), "gdn_a_log": _row(gdn_a_log[l], LANES),
        "gdn_dt_bias": _row(gdn_dt_bias[l], LANES), "gdn_onorm": jnp.tile(_row(gdn_onorm[l]), (1, H_GDN)),
        "w_out": [wo[0:256], wo[256:768], wo[768:1024]],
        "norm_ca": _row(norm_ca[l]), "ca_wq": ca_wq[l].astype(BF16), "ca_wo": ca_wo[l].astype(BF16),
        "ca_qnorm": _row(ca_qnorm[l]), "norm_ffn": _row(norm_ffn[l]),
        "ffn_w1": ffn_w1[l].astype(BF16), "ffn_w3": ffn_w3[l].astype(BF16), "ffn_w2": ffn_w2[l].astype(BF16),
    }


def _forward(x_prompt, x_sample, mem_prompt, cache_k, cache_v, cache_mem_k, cache_mem_v, state_gla, state_gdn,
             state_conv, page_table, norm_mix, w_in, gla_w_a2, gla_b_a, gla_onorm, diff_qnorm, diff_knorm,
             lam_q1, lam_k1, lam_q2, lam_k2, diff_onorm, gdn_conv_w, gdn_a_log, gdn_dt_bias, gdn_onorm, w_out,
             norm_ca, norm_mem, ca_wq, ca_wk, ca_wv, ca_wo, ca_qnorm, ca_knorm, norm_ffn, ffn_w1, ffn_w3, ffn_w2,
             *, chunk_p, tm_p, tq_p, rows_s, pages):
    depth = w_in.shape[0]
    bp, tp, _ = x_prompt.shape
    bs, ts, _ = x_sample.shape
    n_mem = mem_prompt.shape[1]
    slopes = 2.0 ** (-8.0 * jnp.arange(1, H_DIFF + 1, dtype=F32) / H_DIFF)
    consts_p = _consts(chunk_p)
    consts_s = _consts(rows_s)
    p64_512 = _block_ones(DIFF_W, DK_DIFF).astype(BF16)

    xp = x_prompt.reshape(bp * tp, D_MODEL)
    xs = jnp.pad(x_sample, ((0, 0), (0, rows_s - ts), (0, 0))).reshape(bs * rows_s, D_MODEL)
    memp = mem_prompt.reshape(bp * n_mem, D_MODEL)
    zeros_gla = jnp.zeros((bp, H_GLA, DK_GLA, DV_GLA), F32)
    zeros_gdn = jnp.zeros((bp, H_GDN, DK_GDN, DV_GDN), F32)
    zeros_conv = jnp.zeros((bp, CONV_W - 1, C_CONV), F32)
    tm_m = min(bp * n_mem, 512)
    tm_s = bs * rows_s
    cache_k = cache_k.reshape(cache_k.shape[0], cache_k.shape[1], PAGE_SIZE * H_DIFF, LANES)
    cache_v = cache_v.reshape(cache_v.shape[0], cache_v.shape[1], PAGE_SIZE * H_DIFF, LANES)

    mem_k_s = cache_mem_k.reshape(depth, bs, n_mem * H_CA, DH_CA)
    mem_v_s = cache_mem_v.reshape(depth, bs, n_mem * H_CA, DH_CA)
    kv_p = (jnp.zeros((depth, bp * tp, DIFF_W), F32), jnp.zeros((depth, bp * tp, DIFF_W), F32))
    kv_s = (jnp.zeros((depth, bs * rows_s, DIFF_W), F32), jnp.zeros((depth, bs * rows_s, DIFF_W), F32))

    outs = [[] for _ in range(8)]
    for l in range(depth):
        wts = _layer_weights(l, norm_mix, w_in, gla_w_a2, gla_b_a, gla_onorm, diff_qnorm, diff_knorm, lam_q1,
                             lam_k1, lam_q2, lam_k2, diff_onorm, gdn_conv_w, gdn_a_log, gdn_dt_bias, gdn_onorm,
                             w_out, norm_ca, ca_wq, ca_wo, ca_qnorm, norm_ffn, ffn_w1, ffn_w3, ffn_w2)
        mk, mk16 = _norm_mm(memp, _row(norm_mem[l]), ca_wk[l].astype(BF16), _row(ca_knorm[l]), tm_m, headnorm=True,
                            scale=1.0, out_dtype=F32, head_major_copy=True)
        mv, mv16 = _norm_mm(memp, _row(norm_mem[l]), ca_wv[l].astype(BF16), _row(ca_knorm[l]), tm_m, headnorm=False,
                            scale=1.0, out_dtype=F32, head_major_copy=True)

        def diff_p(q16, k16, v16, vt16, lamp, onorm, lam_init):
            return _diff_prompt(q16, k16, vt16, slopes, lamp, onorm, bp, tp, tq_p, lam_init)

        def ca_p(qc, mk16=mk16, mv16=mv16):
            spec = pl.BlockSpec((H_CA, n_mem, DH_CA), lambda b, i: (0, b, 0))
            return _ca_attn(qc, mk16, mv16, bp, tp, min(tp, 512), spec, False)

        xp, kv_p, sgp, sdp, cvp = _layer(xp, wts, l, depth, kv_p, bp, tp, chunk_p, chunk_p, tm_p, zeros_gla,
                                         zeros_gdn, zeros_conv, ca_p, diff_p, consts_p, p64_512)

        def diff_s(q16, k16, v16, vt16, lamp, onorm, lam_init, l=l):
            return _diff_decode(q16, k16, v16, cache_k, cache_v, page_table, slopes, lamp, onorm, l, bs, rows_s,
                                ts, lam_init, pages)

        def ca_s(qc, l=l):
            spec = pl.BlockSpec((None, None, n_mem * H_CA, DH_CA), lambda b, i: (l, b, 0, 0))
            return _ca_attn(qc, mem_k_s, mem_v_s, bs, rows_s, rows_s, spec, True)

        xs, kv_s, sgs, sds, cvs = _layer(xs, wts, l, depth, kv_s, bs, rows_s, rows_s, ts, tm_s, state_gla[l],
                                         state_gdn[l], state_conv[l], ca_s, diff_s, consts_s, p64_512)
        vals = (sgp, sgs, sdp, sds, cvp, cvs, mk.reshape(bp, n_mem, H_CA, DH_CA), mv.reshape(bp, n_mem, H_CA, DH_CA))
        for o, v in zip(outs, vals):
            o.append(v)

    y_p = xp.reshape(bp, tp, D_MODEL)
    y_s = xs.reshape(bs, rows_s, D_MODEL)[:, :ts]
    kv_out = (kv_p[0].reshape(depth, bp, tp, H_DIFF, 2 * DK_DIFF), kv_p[1].reshape(depth, bp, tp, H_DIFF, DV_DIFF),
              kv_s[0].reshape(depth, bs, rows_s, H_DIFF, 2 * DK_DIFF)[:, :, :ts],
              kv_s[1].reshape(depth, bs, rows_s, H_DIFF, DV_DIFF)[:, :, :ts])
    return (y_p, y_s) + kv_out + tuple(jnp.stack(o) for o in outs)


def kernel(x_prompt, x_sample, mem_prompt, cache_k, cache_v, cache_mem_k, cache_mem_v, state_gla, state_gdn, state_conv, page_table, norm_mix, w_in, gla_w_a2, gla_b_a, gla_onorm, diff_qnorm, diff_knorm, lam_q1, lam_k1, lam_q2, lam_k2, diff_onorm, gdn_conv_w, gdn_a_log, gdn_dt_bias, gdn_onorm, w_out, norm_ca, norm_mem, ca_wq, ca_wk, ca_wv, ca_wo, ca_qnorm, ca_knorm, norm_ffn, ffn_w1, ffn_w3, ffn_w2):
    return _forward(x_prompt, x_sample, mem_prompt, cache_k, cache_v, cache_mem_k, cache_mem_v, state_gla,
                    state_gdn, state_conv, page_table, norm_mix, w_in, gla_w_a2, gla_b_a, gla_onorm, diff_qnorm,
                    diff_knorm, lam_q1, lam_k1, lam_q2, lam_k2, diff_onorm, gdn_conv_w, gdn_a_log, gdn_dt_bias,
                    gdn_onorm, w_out, norm_ca, norm_mem, ca_wq, ca_wk, ca_wv, ca_wo, ca_qnorm, ca_knorm, norm_ffn,
                    ffn_w1, ffn_w3, ffn_w2, chunk_p=64, tm_p=512, tq_p=512, rows_s=16, pages=16)
```

```python
import functools
import math

import jax
import jax.numpy as jnp
from jax import lax
from jax.experimental import pallas as pl
from jax.experimental.pallas import tpu as pltpu

F32 = jnp.float32
BF16 = jnp.bfloat16

D_MODEL = 1024
DEPTH = 4
PAGE_SIZE = 128
H_GLA, DK_GLA, DV_GLA, GLA_RANK, GLA_TAU = 4, 32, 64, 16, 16.0
H_DIFF, DK_DIFF, DV_DIFF = 4, 64, 128
H_GDN, DK_GDN, DV_GDN, CONV_W = 4, 64, 64, 4
C_CONV = 2 * H_GDN * DK_GDN + H_GDN * DV_GDN
N_MEM, H_CA = 256, 4
DH_CA = D_MODEL // H_CA
D_FF = -(-8 * D_MODEL // (3 * 256)) * 256
RMS_EPS = 1e-6
L2_EPS = 1e-6

VMEM_LIMIT_V7X = 56 * 1024 * 1024
LANES = 128

GLA_W = 896
GDN_W = 1152
DIFF_W = 512
OFF_GLA, OFF_DQ, OFF_DK, OFF_DV, OFF_GDN = 0, 896, 1408, 1920, 2432
IN_W = OFF_GDN + GDN_W

NEG_BIG = -1e30


def _dot(a, b):
    return jnp.dot(a, b, preferred_element_type=F32)


def _dot_nt(a, b):
    return lax.dot_general(a, b, (((1,), (1,)), ((), ())), preferred_element_type=F32)


def _hilo(a):
    hi = a.astype(BF16)
    lo = (a - hi.astype(F32)).astype(BF16)
    return hi, lo


def _dot_sel_rhs(a, sel):
    hi, lo = _hilo(a)
    return _dot(hi, sel) + _dot(lo, sel)


def _dot_sel_lhs(sel, b):
    hi, lo = _hilo(b)
    return _dot(sel, hi) + _dot(sel, lo)


def _dot3(a, b):
    ah, al = _hilo(a)
    bh, bl = _hilo(b)
    return _dot(ah, bh) + _dot(ah, bl) + _dot(al, bh)


def _sigmoid(x):
    return 1.0 / (1.0 + jnp.exp(-x))


def _softplus(x):
    return jnp.maximum(x, 0.0) + jnp.log(1.0 + jnp.exp(-jnp.abs(x)))


def _rms_rows(x, g):
    ms = jnp.mean(x * x, axis=-1, keepdims=True)
    return x * lax.rsqrt(ms + RMS_EPS) * g


def _pad_rows_128(a):
    r = a.shape[0]
    if r == LANES:
        return a
    return jnp.concatenate([a, jnp.zeros((LANES - r, a.shape[1]), a.dtype)], axis=0)


def _const_spec(shape):
    nd = len(shape)
    return pl.BlockSpec(shape, lambda *_: (0,) * nd)


def _params(sem):
    return pltpu.CompilerParams(dimension_semantics=sem, vmem_limit_bytes=VMEM_LIMIT_V7X)


def _in_proj_kernel(x_ref, g_ref, w_ref, qkg_ref, p64_ref, *rest):
    gla_ref, q_ref, k_ref, kb_ref, v_ref, vb_ref, vt_ref, gdn_ref = rest[-8:]
    xn = _rms_rows(x_ref[...], g_ref[...]).astype(BF16)
    gla_ref[...] = _dot(xn, w_ref[:, OFF_GLA:OFF_GLA + GLA_W])

    def qknorm(y, gain):
        ss = _dot_sel_rhs(y * y, p64_ref[...])
        return y * lax.rsqrt(ss * (1.0 / DK_DIFF) + RMS_EPS) * gain

    yq = _dot(xn, w_ref[:, OFF_DQ:OFF_DQ + DIFF_W])
    q_ref[...] = qknorm(yq, qkg_ref[:, 0:DIFF_W]).astype(BF16)
    yk = _dot(xn, w_ref[:, OFF_DK:OFF_DK + DIFF_W])
    kk = qknorm(yk, qkg_ref[:, DIFF_W:2 * DIFF_W])
    kb_ref[...] = kk.astype(BF16)
    yv = _dot(xn, w_ref[:, OFF_DV:OFF_DV + DIFF_W])
    vb_ref[...] = yv.astype(BF16)
    tm = kk.shape[0]
    for h in range(H_DIFF):
        k_ref[pl.ds(h, tm, stride=H_DIFF), :] = kk[:, h * LANES:(h + 1) * LANES]
        v_ref[pl.ds(h, tm, stride=H_DIFF), :] = yv[:, h * LANES:(h + 1) * LANES]
    vt_ref[...] = jnp.transpose(yv).astype(BF16)
    gdn_ref[...] = _dot(xn, w_ref[:, OFF_GDN:OFF_GDN + GDN_W])


def _in_proj(x, g, w, qkg, p64, tm, layer, depth, stacks):
    m = x.shape[0]
    row = lambda wd: pl.BlockSpec((tm, wd), lambda i: (i, 0))
    slot = pl.BlockSpec((None, tm * H_DIFF, LANES), lambda i: (layer, i, 0))
    stack_shape = jax.ShapeDtypeStruct((depth, m * H_DIFF, LANES), F32)
    in_specs = [row(D_MODEL), _const_spec((1, D_MODEL)), _const_spec((D_MODEL, IN_W)),
                _const_spec((1, 2 * DIFF_W)), _const_spec((DIFF_W, DIFF_W))]
    args = [x, g, w, qkg, p64]
    aliases = {}
    if stacks is not None:
        in_specs += [pl.BlockSpec(memory_space=pl.ANY), pl.BlockSpec(memory_space=pl.ANY)]
        args += list(stacks)
        aliases = {5: 2, 6: 4}
    return pl.pallas_call(
        _in_proj_kernel,
        grid=(m // tm,),
        in_specs=in_specs,
        out_specs=[row(GLA_W), row(DIFF_W), slot, row(DIFF_W), slot, row(DIFF_W),
                   pl.BlockSpec((DIFF_W, tm), lambda i: (0, i)), row(GDN_W)],
        out_shape=[jax.ShapeDtypeStruct((m, GLA_W), F32), jax.ShapeDtypeStruct((m, DIFF_W), BF16),
                   stack_shape, jax.ShapeDtypeStruct((m, DIFF_W), BF16),
                   stack_shape, jax.ShapeDtypeStruct((m, DIFF_W), BF16),
                   jax.ShapeDtypeStruct((DIFF_W, m), BF16), jax.ShapeDtypeStruct((m, GDN_W), F32)],
        input_output_aliases=aliases,
        compiler_params=_params(("parallel",)),
        name="in_proj",
    )(*args)


def _norm_mm_kernel(x_ref, g_ref, w_ref, hg_ref, o_ref, *maybe_heads_ref, headnorm, scale):
    xn = _rms_rows(x_ref[...], g_ref[...]).astype(BF16)
    y = _dot(xn, w_ref[...])
    for h in range(H_CA):
        yh = y[:, h * DH_CA:(h + 1) * DH_CA]
        if headnorm:
            yh = _rms_rows(yh, hg_ref[...]) * scale
        o_ref[:, h * DH_CA:(h + 1) * DH_CA] = yh.astype(o_ref.dtype)
        if maybe_heads_ref:
            maybe_heads_ref[0][h] = yh.astype(BF16)


def _norm_mm(x, g, w, hg, tm, *, headnorm, scale, out_dtype, head_major_copy=False):
    m, n = x.shape[0], w.shape[1]
    out_specs = [pl.BlockSpec((tm, n), lambda i: (i, 0))]
    out_shape = [jax.ShapeDtypeStruct((m, n), out_dtype)]
    if head_major_copy:
        out_specs.append(pl.BlockSpec((H_CA, tm, DH_CA), lambda i: (0, i, 0)))
        out_shape.append(jax.ShapeDtypeStruct((H_CA, m, DH_CA), BF16))
    outs = pl.pallas_call(
        functools.partial(_norm_mm_kernel, headnorm=headnorm, scale=scale),
        grid=(m // tm,),
        in_specs=[pl.BlockSpec((tm, D_MODEL), lambda i: (i, 0)), _const_spec((1, D_MODEL)),
                  _const_spec((D_MODEL, n)), _const_spec((1, DH_CA))],
        out_specs=out_specs,
        out_shape=out_shape,
        compiler_params=_params(("parallel",)),
        name="norm_mm",
    )(x, g, w, hg)
    return outs if head_major_copy else outs[0]


def _mm_res_kernel(*refs, n):
    xs, ws, res_ref, o_ref = refs[:n], refs[n:2 * n], refs[2 * n], refs[2 * n + 1]
    acc = res_ref[...]
    for x_ref, w_ref in zip(xs, ws):
        acc = acc + _dot(x_ref[...], w_ref[...])
    o_ref[...] = acc


def _mm_res(xs, ws, res, tm):
    m, n = res.shape
    k = len(xs)
    in_specs = ([pl.BlockSpec((tm, x.shape[1]), lambda i: (i, 0)) for x in xs]
                + [_const_spec(w.shape) for w in ws]
                + [pl.BlockSpec((tm, n), lambda i: (i, 0))])
    return pl.pallas_call(
        functools.partial(_mm_res_kernel, n=k),
        grid=(m // tm,),
        in_specs=in_specs,
        out_specs=pl.BlockSpec((tm, n), lambda i: (i, 0)),
        out_shape=jax.ShapeDtypeStruct((m, n), F32),
        compiler_params=_params(("parallel",)),
        name="mm_res",
    )(*xs, *ws, res)


FF_CHUNK = D_FF // 2


def _swiglu_kernel(x_ref, g_ref, w1_ref, w3_ref, o_ref):
    xn = _rms_rows(x_ref[...], g_ref[...]).astype(BF16)
    for c in range(D_FF // FF_CHUNK):
        sl = slice(c * FF_CHUNK, (c + 1) * FF_CHUNK)
        a = _dot(xn, w1_ref[:, sl])
        b = _dot(xn, w3_ref[:, sl])
        o_ref[:, sl] = (a * _sigmoid(a) * b).astype(BF16)


def _swiglu(x, g, w1, w3, tm):
    m = x.shape[0]
    return pl.pallas_call(
        _swiglu_kernel,
        grid=(m // tm,),
        in_specs=[pl.BlockSpec((tm, D_MODEL), lambda i: (i, 0)), _const_spec((1, D_MODEL)),
                  _const_spec((D_MODEL, D_FF)), _const_spec((D_MODEL, D_FF))],
        out_specs=pl.BlockSpec((tm, D_FF), lambda i: (i, 0)),
        out_shape=jax.ShapeDtypeStruct((m, D_FF), BF16),
        compiler_params=_params(("parallel",)),
        name="swiglu",
    )(x, g, w1, w3)


GLA_SB = 16


def _gla_chunk(p, s, wa_ref, ba_ref, on_ref, cum_ref, seg_ref, p64_ref, bd_ref, b_s, q_s, k_s, v_s, tall_s,
               *, chunk, n_valid):
    hk, hv = H_GLA * DK_GLA, H_GLA * DV_GLA
    nsb = chunk // GLA_SB
    q = p[:, 0:128] * (DK_GLA ** -0.5)
    k = p[:, 128:256]
    v = p[:, 256:512]
    gate = p[:, 512:768]
    x = _dot(p[:, 768:896].astype(BF16), wa_ref[...]) + ba_ref[...]
    la = (jnp.minimum(x, 0.0) - jnp.log(1.0 + jnp.exp(-jnp.abs(x)))) * (1.0 / GLA_TAU)
    if n_valid < chunk:
        rowv = lax.broadcasted_iota(jnp.int32, (chunk, 1), 0) < n_valid
        la = jnp.where(rowv, la, 0.0)
        k = jnp.where(rowv, k, 0.0)
    cums = _dot_sel_lhs(cum_ref[...], la)
    b = cums[0:chunk]
    b_s[...] = b
    q_s[...] = q
    k_s[...] = k
    v_s[...] = v
    v16 = v.astype(BF16)
    o = _dot((q * jnp.exp(b)).astype(BF16), s.astype(BF16))

    if nsb > 1:
        mq, mqc = cums[chunk:2 * chunk], cums[2 * chunk:3 * chunk]
        mh, mhc = cums[3 * chunk:4 * chunk], cums[4 * chunk:5 * chunk]
        lane = lax.broadcasted_iota(jnp.int32, (1, hk), 1)

        def stacked(a):
            return jnp.concatenate(
                [jnp.where((lane >= h * DK_GLA) & (lane < (h + 1) * DK_GLA), a, 0.0) for h in range(H_GLA)],
                axis=0).astype(BF16)

        a2 = _dot_nt(stacked(q * jnp.exp(b - mq)), (k * jnp.exp(mqc - b)).astype(BF16))
        a1 = _dot_nt(stacked(q * jnp.exp(b - mh)), (k * jnp.exp(mhc - b)).astype(BF16))
        qi = (lax.broadcasted_iota(jnp.int32, (H_GLA * chunk, chunk), 0) & (chunk - 1)) // GLA_SB
        qj = lax.broadcasted_iota(jnp.int32, (H_GLA * chunk, chunk), 1) // GLA_SB
        att = jnp.where(qi == qj + 1, a2, jnp.where(qi > qj + 1, a1, 0.0))
        r = _dot(att.astype(BF16), v16)
        lane_v = lax.broadcasted_iota(jnp.int32, (1, hv), 1)
        for h in range(H_GLA):
            o = o + jnp.where((lane_v >= h * DV_GLA) & (lane_v < (h + 1) * DV_GLA),
                              r[h * chunk:(h + 1) * chunk], 0.0)

    row = lax.broadcasted_iota(jnp.int32, (GLA_SB, 1), 0)
    for j in range(GLA_SB):
        for d in range(nsb):
            sl = slice(d * GLA_SB, (d + 1) * GLA_SB)
            bj = b_s[d * GLA_SB + j:d * GLA_SB + j + 1, :]
            kj = k_s[d * GLA_SB + j:d * GLA_SB + j + 1, :]
            e = jnp.exp(jnp.where(row >= j, b_s[sl, :] - bj, -jnp.inf))
            tall_s[j * chunk + d * GLA_SB:j * chunk + (d + 1) * GLA_SB, :] = (q_s[sl, :] * kj * e).astype(BF16)
    tsum = _dot(tall_s[...], seg_ref[...])
    for j in range(GLA_SB):
        vj = jnp.concatenate(
            [jnp.broadcast_to(v_s[d * GLA_SB + j:d * GLA_SB + j + 1, :], (GLA_SB, hv)) for d in range(nsb)], axis=0)
        o = o + tsum[j * chunk:(j + 1) * chunk] * vj

    b_end = b[chunk - 1:chunk, :]
    kd = k * jnp.exp(b_end - b)
    kd_t = jnp.transpose(_pad_rows_128(kd)).astype(BF16)
    upd = _dot(kd_t, _pad_rows_128(v).astype(BF16))
    dcol = jnp.transpose(jnp.broadcast_to(jnp.exp(b_end), (LANES, LANES)))[:, 0:1]
    s_new = s * dcol + upd * bd_ref[...]

    ss = _dot_sel_rhs(o * o, p64_ref[...])
    on = o * lax.rsqrt(ss * (1.0 / DV_GLA) + RMS_EPS) * on_ref[...]
    return on * gate * _sigmoid(gate), s_new


def _gla_kernel(p_ref, s0_ref, wa_ref, ba_ref, on_ref, cum_ref, seg_ref, p64_ref, bd_ref,
                o_ref, sout_ref, s_s, b_s, q_s, k_s, v_s, tall_s, *, chunk, n_chunks, n_valid):
    c = pl.program_id(1)

    @pl.when(c == 0)
    def _():
        s_s[...] = s0_ref[0]

    step = functools.partial(_gla_chunk, wa_ref=wa_ref, ba_ref=ba_ref, on_ref=on_ref, cum_ref=cum_ref,
                             seg_ref=seg_ref, p64_ref=p64_ref, bd_ref=bd_ref, b_s=b_s, q_s=q_s, k_s=k_s, v_s=v_s,
                             tall_s=tall_s, chunk=chunk, n_valid=n_valid)

    def body(ci, s):
        r0 = pl.multiple_of(ci * chunk, chunk)
        o, s_new = step(p_ref[pl.ds(r0, chunk), :], s)
        o_ref[pl.ds(r0, chunk), :] = o.astype(BF16)
        return s_new

    s_new = lax.fori_loop(0, n_chunks, body, s_s[...])
    s_s[...] = s_new

    @pl.when(c == pl.num_programs(1) - 1)
    def _():
        sout_ref[0] = s_new


def _gla(proj, s0, wa, ba, onorm, consts, bc, t, chunk, n_valid, block):
    nb = t // block
    hk, hv = H_GLA * DK_GLA, H_GLA * DV_GLA
    cum = consts["gla_cum"]
    return pl.pallas_call(
        functools.partial(_gla_kernel, chunk=chunk, n_chunks=block // chunk, n_valid=n_valid),
        grid=(bc, nb),
        in_specs=[pl.BlockSpec((block, GLA_W), lambda b, c: (b * nb + c, 0)),
                  pl.BlockSpec((1, hk, hv), lambda b, c: (b, 0, 0)),
                  _const_spec((LANES, hk)), _const_spec((1, hk)), _const_spec((1, hv)),
                  _const_spec(cum.shape), _const_spec((hk, hv)), _const_spec((hv, hv)),
                  _const_spec((hk, hv))],
        out_specs=[pl.BlockSpec((block, hv), lambda b, c: (b * nb + c, 0)),
                   pl.BlockSpec((1, hk, hv), lambda b, c: (b, 0, 0))],
        out_shape=[jax.ShapeDtypeStruct((bc * t, hv), BF16), jax.ShapeDtypeStruct((bc, hk, hv), F32)],
        scratch_shapes=[pltpu.VMEM((hk, hv), F32), pltpu.VMEM((chunk, hk), F32), pltpu.VMEM((chunk, hk), F32),
                        pltpu.VMEM((chunk, hk), F32), pltpu.VMEM((chunk, hv), F32),
                        pltpu.VMEM((GLA_SB * chunk, hk), BF16)],
        compiler_params=_params(("parallel", "arbitrary")),
        name="gla",
    )(proj, s0, wa, ba, onorm, cum, consts["seg_gla"], consts["p64_256"], consts["bd_gla"])


def _gdn_kernel(p_ref, s0_ref, cb_ref, cw_ref, alog_ref, dtb_ref, on_ref,
                tri_ref, p64_ref, eg_ref, eg64_ref, eb64_ref, bd_ref,
                o_ref, sout_ref, cout_ref, s_s, xp_s, *, chunk, n_valid, nseq):
    c = pl.program_id(1)
    hd = H_GDN * DK_GDN
    last = c == pl.num_programs(1) - 1

    @pl.when(c == 0)
    def _():
        s_s[...] = s0_ref[...]
        xp_s[:, 5:8, :] = cb_ref[...]

    def l2n(a):
        ss = _dot_sel_rhs(a * a, p64_ref[...])
        return a * lax.rsqrt(ss + L2_EPS)

    ri = lax.broadcasted_iota(jnp.int32, (chunk, chunk), 0)
    ci = lax.broadcasted_iota(jnp.int32, (chunk, chunk), 1)
    eye = (ri == ci).astype(F32)
    lane = lax.broadcasted_iota(jnp.int32, (1, hd), 1)
    lms = [(lane >= h * DK_GDN) & (lane < (h + 1) * DK_GDN) for h in range(H_GDN)]
    n_sq = int(math.log2(chunk)) - 1

    pre = []
    for n in range(nseq):
        p = p_ref[n]
        xp_s[n, 8:8 + chunk, :] = p[:, 0:C_CONV]
        y = xp_s[n, pl.ds(5, chunk), :] * cw_ref[0:1, :]
        for i in range(1, CONV_W):
            y = y + xp_s[n, pl.ds(5 + i, chunk), :] * cw_ref[i:i + 1, :]
        y = y * _sigmoid(y)
        tail = xp_s[n, pl.ds(5 + n_valid, CONV_W - 1), :]
        xp_s[n, 5:8, :] = tail

        @pl.when(last)
        def _(n=n, tail=tail):
            cout_ref[n] = tail

        cq = l2n(y[:, 0:hd]) * (DK_GDN ** -0.5)
        ck = l2n(y[:, hd:2 * hd])
        cv = y[:, 2 * hd:3 * hd]
        small = p[:, 1024:1152]
        g = -jnp.exp(alog_ref[...]) * _softplus(small + dtb_ref[...])
        beta = _sigmoid(small)
        if n_valid < chunk:
            rowv = lax.broadcasted_iota(jnp.int32, (chunk, 1), 0) < n_valid
            g = jnp.where(rowv, g, 0.0)
            beta = jnp.where(rowv, beta, 0.0)
        gam = _dot_sel_lhs(tri_ref[...], g)
        gam_w = _dot_sel_rhs(gam, eg_ref[...])
        gam_b = _dot_sel_rhs(gam, eg64_ref[...])
        beta_b = _dot_sel_rhs(beta, eb64_ref[...])
        gam_t = jnp.transpose(_pad_rows_128(gam))
        eg = jnp.exp(gam_b)
        kb = ck * beta_b
        g_end = gam_b[chunk - 1:chunk, :]
        s = s_s[n]
        pre.append(dict(
            z=p[:, 768:1024], cq=cq, ck16=ck.astype(BF16), eg=eg, kb=kb, kbg=kb * eg, vb=cv * beta_b, g_end=g_end,
            kdec=ck * jnp.exp(g_end - gam_b), s=s, s16=s.astype(BF16),
            decs=[jnp.exp(jnp.where(ri >= ci, gam_w[:, h * LANES:h * LANES + chunk] - gam_t[h:h + 1, 0:chunk],
                                    -jnp.inf)) for h in range(H_GDN)]))

    pairs = [(n, h) for n in range(nseq) for h in range(H_GDN)]
    pws = [-(_dot_nt(jnp.where(lms[h], pre[n]["kb"], 0.0).astype(BF16), pre[n]["ck16"])
             * jnp.where(ri > ci, pre[n]["decs"][h], 0.0)) for n, h in pairs]
    atts = [(_dot_nt(jnp.where(lms[h], pre[n]["cq"], 0.0).astype(BF16), pre[n]["ck16"])
             * pre[n]["decs"][h]).astype(BF16) for n, h in pairs]
    tinvs = [eye + pw for pw in pws]
    for _ in range(n_sq):
        pws = [_dot3(pw, pw) for pw in pws]
        tinvs = [ti + _dot3(ti, pw) for ti, pw in zip(tinvs, pws)]
    sols = [_dot3(ti, jnp.concatenate([jnp.where(lms[h], pre[n]["vb"], 0.0), jnp.where(lms[h], pre[n]["kbg"], 0.0)],
                                      axis=1)) for ti, (n, h) in zip(tinvs, pairs)]

    def seq_sum(n, lo):
        parts = [sols[n * H_GDN + h][:, lo:lo + hd] for h in range(H_GDN)]
        return (parts[0] + parts[1]) + (parts[2] + parts[3])

    v_news = [seq_sum(n, 0) - _dot(seq_sum(n, hd).astype(BF16), pre[n]["s16"]) for n in range(nseq)]
    outs = [_dot((pre[n]["cq"] * pre[n]["eg"]).astype(BF16), pre[n]["s16"]) for n in range(nseq)]
    intra = [_dot(atts[i], jnp.where(lms[h], v_news[n], 0.0).astype(BF16)) for i, (n, h) in enumerate(pairs)]
    upds = [_dot(jnp.transpose(_pad_rows_128(pre[n]["kdec"])).astype(BF16), _pad_rows_128(v_news[n]).astype(BF16))
            for n in range(nseq)]
    for n in range(nseq):
        o = outs[n]
        for h in range(H_GDN):
            o = o + intra[n * H_GDN + h]
        s_new = pre[n]["s"] * jnp.exp(pre[n]["g_end"]) + upds[n] * bd_ref[...]
        s_s[n] = s_new
        ss = _dot_sel_rhs(o * o, p64_ref[...])
        on = o * lax.rsqrt(ss * (1.0 / DV_GDN) + RMS_EPS) * on_ref[...]
        z = pre[n]["z"]
        o_ref[n] = (on * z * _sigmoid(z)).astype(BF16)

        @pl.when(last)
        def _(n=n, s_new=s_new):
            sout_ref[n] = s_new


def _gdn(proj, s0, conv0, cw, alog, dtb, onorm, consts, bc, t, chunk, n_valid, nseq):
    nc = t // chunk
    hd = H_GDN * DK_GDN
    assert bc % nseq == 0
    o, s_out, conv_out = pl.pallas_call(
        functools.partial(_gdn_kernel, chunk=chunk, n_valid=n_valid, nseq=nseq),
        grid=(bc // nseq, nc),
        in_specs=[pl.BlockSpec((nseq, chunk, GDN_W), lambda b, c: (b, c, 0)),
                  pl.BlockSpec((nseq, hd, hd), lambda b, c: (b, 0, 0)),
                  pl.BlockSpec((nseq, CONV_W - 1, C_CONV), lambda b, c: (b, 0, 0)),
                  _const_spec((CONV_W, C_CONV)), _const_spec((1, LANES)), _const_spec((1, LANES)),
                  _const_spec((1, hd)),
                  _const_spec((chunk, chunk)), _const_spec((hd, hd)), _const_spec((LANES, 4 * LANES)),
                  _const_spec((LANES, hd)), _const_spec((LANES, hd)), _const_spec((hd, hd))],
        out_specs=[pl.BlockSpec((nseq, chunk, hd), lambda b, c: (b, c, 0)),
                   pl.BlockSpec((nseq, hd, hd), lambda b, c: (b, 0, 0)),
                   pl.BlockSpec((nseq, CONV_W - 1, C_CONV), lambda b, c: (b, 0, 0))],
        out_shape=[jax.ShapeDtypeStruct((bc, t, hd), BF16), jax.ShapeDtypeStruct((bc, hd, hd), F32),
                   jax.ShapeDtypeStruct((bc, CONV_W - 1, C_CONV), F32)],
        scratch_shapes=[pltpu.VMEM((nseq, hd, hd), F32), pltpu.VMEM((nseq, chunk + 8, C_CONV), F32)],
        compiler_params=_params(("parallel", "arbitrary")),
        name="gdn",
    )(proj.reshape(bc, t, GDN_W), s0, conv0, cw, alog, dtb, onorm, consts["tri"], consts["p64_256"], consts["eg"],
      consts["eg64"], consts["eb64"], consts["bd_gdn"])
    return o.reshape(bc * t, hd), s_out, conv_out


def _lam_value(lamp_ref, lam_init):
    lp = lamp_ref[...]
    return (jnp.exp(jnp.sum(lp[0:1] * lp[1:2], axis=-1, keepdims=True))
            - jnp.exp(jnp.sum(lp[2:3] * lp[3:4], axis=-1, keepdims=True)) + lam_init)


def _softmax_step(carry, s, v16):
    m, l, acc = carry
    m_new = jnp.maximum(m, jnp.max(s, axis=-1, keepdims=True))
    alpha = jnp.exp(m - m_new)
    p = jnp.exp(s - m_new)
    l = alpha * l + jnp.sum(p, axis=-1, keepdims=True)
    acc = alpha * acc + _dot(p.astype(BF16), v16)
    return m_new, l, acc


def _diff_prompt_kernel(slope_ref, lamp_ref, ong_ref, q_ref, k_ref, vt_ref, o_ref, *, tq, lam_init):
    h = pl.program_id(1)
    qi = pl.program_id(2)
    w = 2 * tq
    slope = slope_ref[h]
    lam = _lam_value(lamp_ref, lam_init)
    krow = lax.broadcasted_iota(jnp.int32, (tq, w), 0)
    base = slope * krow.astype(F32)
    rowc = lax.broadcasted_iota(jnp.int32, (LANES, tq), 0)
    q_t = jnp.transpose(q_ref[...].astype(F32))
    qt = jnp.concatenate([jnp.where(rowc < DK_DIFF, q_t, 0.0), jnp.where(rowc >= DK_DIFF, q_t, 0.0)],
                         axis=1).astype(BF16)

    def block(kj, carry, masked):
        m, l, acc = carry
        off = pl.multiple_of(kj * tq, tq)
        s = _dot(k_ref[pl.ds(off, tq), :], qt) + base
        if masked:
            qpos = lax.broadcasted_iota(jnp.int32, (tq, w), 1) & (tq - 1)
            s = jnp.where(krow <= qpos, s, NEG_BIG)
        c = slope * ((kj - qi) * tq).astype(F32)
        m_new = jnp.maximum(m, jnp.max(s, axis=0, keepdims=True) + c)
        p = jnp.exp(s - (m_new - c))
        alpha = jnp.exp(m - m_new)
        l = alpha * l + jnp.sum(p, axis=0, keepdims=True)
        acc = alpha * acc + _dot(vt_ref[:, pl.ds(off, tq)], p.astype(BF16))
        return m_new, l, acc

    carry = (jnp.full((1, w), NEG_BIG, F32), jnp.zeros((1, w), F32), jnp.zeros((LANES, w), F32))
    carry = lax.fori_loop(0, qi, functools.partial(block, masked=False), carry)
    _, l, acc = block(qi, carry, True)
    o = acc / l
    d = o[:, 0:tq] - lam * o[:, tq:w]
    ms = jnp.mean(d * d, axis=0, keepdims=True)
    dn = d * lax.rsqrt(ms + RMS_EPS) * ong_ref[...] * (1.0 - lam_init)
    o_ref[...] = jnp.transpose(dn).astype(BF16)


def _diff_prompt(q, k16, vt16, slopes, lamp, onorm, bc, t, tq, lam_init):
    nq = t // tq
    assert tq & (tq - 1) == 0 and DV_DIFF == LANES
    ong = jnp.broadcast_to(onorm.reshape(DV_DIFF, 1), (DV_DIFF, tq))
    return pl.pallas_call(
        functools.partial(_diff_prompt_kernel, tq=tq, lam_init=lam_init),
        grid=(bc, H_DIFF, nq),
        in_specs=[pl.BlockSpec(memory_space=pltpu.SMEM), _const_spec((4, DK_DIFF)), _const_spec((DV_DIFF, tq)),
                  pl.BlockSpec((tq, LANES), lambda b, h, i: (b * nq + i, h)),
                  pl.BlockSpec((t, LANES), lambda b, h, i: (b, h)),
                  pl.BlockSpec((LANES, t), lambda b, h, i: (h, b))],
        out_specs=pl.BlockSpec((tq, LANES), lambda b, h, i: (b * nq + i, h)),
        out_shape=jax.ShapeDtypeStruct((bc * t, H_DIFF * DV_DIFF), BF16),
        compiler_params=_params(("parallel", "parallel", "arbitrary")),
        name="diff_prompt",
    )(slopes, lamp, ong, q, k16, vt16)


DEC_GROUP = 4


def _diff_decode_kernel(pt_ref, lamp_ref, on_ref, scol_ref, srow_ref, q_ref, kn_ref, vn_ref, *rest,
                        pages, rows, n_valid, past_len, lam_init):
    k_refs, v_refs = rest[:pages], rest[pages:2 * pages]
    o_ref, qq_s, m_s, l_s, acc_s = rest[2 * pages:]
    del pt_ref
    g = pl.program_id(1)
    qr = 2 * n_valid
    nr = H_DIFF * qr
    prow = PAGE_SIZE * H_DIFF
    rowh = lax.broadcasted_iota(jnp.int32, (nr, 1), 0) // qr

    @pl.when(g == 0)
    def _():
        row = lax.broadcasted_iota(jnp.int32, (rows, 1), 0)
        lane = lax.broadcasted_iota(jnp.int32, (rows, LANES), 1)
        for h in range(H_DIFF):
            q = q_ref[:, h * LANES:(h + 1) * LANES].astype(F32)
            q1 = jnp.where(lane < DK_DIFF, q, 0.0)
            q2 = pltpu.roll(jnp.where(lane >= DK_DIFF, q, 0.0), n_valid, axis=0)
            qq_s[h * qr:(h + 1) * qr, :] = jnp.where(row < n_valid, q1, q2)[0:qr]
        m_s[...] = jnp.full(m_s.shape, NEG_BIG, F32)
        l_s[...] = jnp.zeros(l_s.shape, F32)
        acc_s[...] = jnp.zeros(acc_s.shape, F32)

    qq = qq_s[...].astype(BF16)
    col = lax.broadcasted_iota(jnp.int32, (1, prow), 1)
    tcol = (col // H_DIFF).astype(F32)
    own = rowh == (col & (H_DIFF - 1))
    scol = scol_ref[...]
    groups = range(0, pages, DEC_GROUP)
    s_g = []
    for i0 in groups:
        ss = []
        for i in range(i0, i0 + DEC_GROUP):
            s = _dot_nt(qq, k_refs[i][...].astype(BF16))
            base = ((g * pages + i) * PAGE_SIZE - past_len).astype(F32)
            ss.append(jnp.where(own, s + scol * (tcol + base), NEG_BIG))
        s_g.append(jnp.concatenate(ss, axis=1))
    m_g = [jnp.max(s, axis=-1, keepdims=True) for s in s_g]
    p_g = [jnp.exp(s - m) for s, m in zip(s_g, m_g)]
    l_g = [jnp.sum(p, axis=-1, keepdims=True) for p in p_g]
    pv_g = []
    for i0, p32 in zip(groups, p_g):
        p = p32.astype(BF16)
        pv = _dot(p[:, 0:prow], v_refs[i0][...].astype(BF16))
        for i in range(1, DEC_GROUP):
            pv = pv + _dot(p[:, i * prow:(i + 1) * prow], v_refs[i0 + i][...].astype(BF16))
        pv_g.append(pv)
    m_prev = m_s[:, 0:1]
    m_new = m_prev
    for m in m_g:
        m_new = jnp.maximum(m_new, m)
    alpha = jnp.exp(m_prev - m_new)
    l_new = alpha * l_s[:, 0:1]
    acc_new = alpha * acc_s[...]
    for m, l, pv in zip(m_g, l_g, pv_g):
        a = jnp.exp(m - m_new)
        l_new = l_new + a * l
        acc_new = acc_new + a * pv
    m_s[...] = jnp.broadcast_to(m_new, m_s.shape)
    l_s[...] = jnp.broadcast_to(l_new, l_s.shape)
    acc_s[...] = acc_new

    @pl.when(g == pl.num_programs(1) - 1)
    def _():
        coln = lax.broadcasted_iota(jnp.int32, (1, rows), 1)
        tok = lax.broadcasted_iota(jnp.int32, (nr, 1), 0) & (n_valid - 1)
        s = jnp.zeros((nr, rows), F32)
        for h in range(H_DIFF):
            s = jnp.where(rowh == h, _dot_nt(qq, kn_ref[:, h * LANES:(h + 1) * LANES]), s)
        s = s + srow_ref[...] * coln.astype(F32)
        s = jnp.where((coln < n_valid) & (coln <= tok), s, NEG_BIG)
        m_fin = jnp.maximum(m_new, jnp.max(s, axis=-1, keepdims=True))
        alpha = jnp.exp(m_new - m_fin)
        p32 = jnp.exp(s - m_fin)
        l_fin = alpha * l_new + jnp.sum(p32, axis=-1, keepdims=True)
        p = p32.astype(BF16)
        acc = alpha * acc_new
        for h in range(H_DIFF):
            acc = acc + jnp.where(rowh == h, _dot(p, vn_ref[:, h * LANES:(h + 1) * LANES]), 0.0)
        o = acc / l_fin
        d = o - _lam_value(lamp_ref, lam_init) * pltpu.roll(o, nr - n_valid, axis=0)
        d = _rms_rows(d, on_ref[...]) * (1.0 - lam_init)
        row8 = lax.broadcasted_iota(jnp.int32, (qr, 1), 0)
        for h in range(H_DIFF):
            blk = jnp.where(row8 < n_valid, d[h * qr:(h + 1) * qr], 0.0)
            o_ref[:, h * LANES:(h + 1) * LANES] = jnp.concatenate(
                [blk, jnp.zeros((rows - qr, LANES), F32)], axis=0).astype(BF16)


def _diff_decode(q, kn16, vn16, cache_k, cache_v, page_table, slopes, lamp, onorm, layer, bc, rows, n_valid,
                 lam_init, pages):
    n_pages = page_table.shape[1]
    npg = n_pages // pages
    past_len = n_pages * PAGE_SIZE
    width = H_DIFF * LANES
    prow = PAGE_SIZE * H_DIFF
    nr = H_DIFF * 2 * n_valid
    assert pages % DEC_GROUP == 0 and n_valid & (n_valid - 1) == 0 and 2 * n_valid <= rows
    scol = jnp.tile(slopes, PAGE_SIZE).reshape(1, prow)
    srow = jnp.repeat(slopes, 2 * n_valid).reshape(nr, 1)

    def page_spec(i):
        return pl.BlockSpec((None, None, prow, LANES), lambda b, g, pt: (layer, pt[b, g * pages + i], 0, 0))

    tile = pl.BlockSpec((rows, width), lambda b, g, pt: (b, 0))
    grid_spec = pltpu.PrefetchScalarGridSpec(
        num_scalar_prefetch=1,
        grid=(bc, npg),
        in_specs=([pl.BlockSpec((4, DK_DIFF), lambda b, g, pt: (0, 0)),
                   pl.BlockSpec((1, DV_DIFF), lambda b, g, pt: (0, 0)),
                   pl.BlockSpec((1, prow), lambda b, g, pt: (0, 0)),
                   pl.BlockSpec((nr, 1), lambda b, g, pt: (0, 0)),
                   tile, tile, tile]
                  + [page_spec(i) for i in range(pages)] + [page_spec(i) for i in range(pages)]),
        out_specs=tile,
        scratch_shapes=[pltpu.VMEM((nr, LANES), F32), pltpu.VMEM((nr, LANES), F32),
                        pltpu.VMEM((nr, LANES), F32), pltpu.VMEM((nr, LANES), F32)],
    )
    return pl.pallas_call(
        functools.partial(_diff_decode_kernel, pages=pages, rows=rows, n_valid=n_valid, past_len=past_len,
                          lam_init=lam_init),
        grid_spec=grid_spec,
        out_shape=jax.ShapeDtypeStruct((bc * rows, width), BF16),
        compiler_params=_params(("parallel", "arbitrary")),
        name="diff_decode",
    )(page_table, lamp, onorm, scol, srow, q, kn16, vn16, *([cache_k] * pages), *([cache_v] * pages))


def _ca_kernel(q_ref, k_ref, v_ref, o_ref, *, interleaved):
    tq = q_ref.shape[0]
    if interleaved:
        colh = lax.broadcasted_iota(jnp.int32, (1, N_MEM * H_CA), 1) & (H_CA - 1)
        rowh = lax.broadcasted_iota(jnp.int32, (H_CA * tq, 1), 0) // tq
        qs = jnp.concatenate([q_ref[:, h * DH_CA:(h + 1) * DH_CA] for h in range(H_CA)], axis=0)
        s = jnp.where(rowh == colh, _dot_nt(qs, k_ref[...].astype(BF16)), NEG_BIG)
        p = jnp.exp(s - jnp.max(s, axis=-1, keepdims=True))
        a = p / jnp.sum(p, axis=-1, keepdims=True)
        o = _dot(a.astype(BF16), v_ref[...].astype(BF16))
        for h in range(H_CA):
            o_ref[:, h * DH_CA:(h + 1) * DH_CA] = o[h * tq:(h + 1) * tq].astype(BF16)
        return
    for h in range(H_CA):
        hs = slice(h * DH_CA, (h + 1) * DH_CA)
        s = _dot_nt(q_ref[:, hs], k_ref[h])
        p = jnp.exp(s - jnp.max(s, axis=-1, keepdims=True))
        a = p / jnp.sum(p, axis=-1, keepdims=True)
        o_ref[:, hs] = _dot(a.astype(BF16), v_ref[h]).astype(BF16)


def _ca_attn(q, mem_k, mem_v, bc, t, tq, mem_spec, interleaved):
    nq = t // tq
    return pl.pallas_call(
        functools.partial(_ca_kernel, interleaved=interleaved),
        grid=(bc, nq),
        in_specs=[pl.BlockSpec((tq, D_MODEL), lambda b, i: (b * nq + i, 0)), mem_spec, mem_spec],
        out_specs=pl.BlockSpec((tq, D_MODEL), lambda b, i: (b * nq + i, 0)),
        out_shape=jax.ShapeDtypeStruct((bc * t, D_MODEL), BF16),
        compiler_params=_params(("parallel", "arbitrary")),
        name="ca_attn",
    )(q, mem_k, mem_v)


def _block_ones(n, blk):
    i = jnp.arange(n) // blk
    return (i[:, None] == i[None, :])


def _consts(chunk):
    hk, hv = H_GLA * DK_GLA, H_GLA * DV_GLA
    r = jnp.arange(LANES)
    c512 = jnp.arange(4 * LANES)
    c256 = jnp.arange(256)
    i = jnp.arange(chunk)[:, None]
    t = jnp.arange(chunk)[None, :]
    cums = [t <= i]
    if chunk > GLA_SB:
        assert chunk == 4 * GLA_SB
        half = chunk // 2
        cums += [t < GLA_SB * (i // GLA_SB), t <= GLA_SB * (i // GLA_SB) + GLA_SB - 1,
                 t < half * (i // half), t <= half * (i // half) + half - 1]
    return {
        "gla_cum": jnp.concatenate(cums, axis=0).astype(BF16),
        "tri": jnp.tril(jnp.ones((chunk, chunk), F32)).astype(BF16),
        "seg_gla": ((jnp.arange(hk) // DK_GLA)[:, None] == (jnp.arange(hv) // DV_GLA)[None, :]).astype(BF16),
        "bd_gla": ((jnp.arange(hk) // DK_GLA)[:, None] == (jnp.arange(hv) // DV_GLA)[None, :]).astype(F32),
        "p64_256": _block_ones(256, 64).astype(BF16),
        "bd_gdn": _block_ones(256, 64).astype(F32),
        "eg": (r[:, None] == (c512 // LANES)[None, :]).astype(BF16),
        "eg64": (r[:, None] == (c256 // 64)[None, :]).astype(BF16),
        "eb64": (r[:, None] == (c256 // 64 + H_GDN)[None, :]).astype(BF16),
    }


def _expand_state(s, bd):
    b, h, dk, dv = s.shape
    return jnp.tile(s.reshape(b, h * dk, dv), (1, 1, h)) * bd


def _compact_state(s, h):
    b, hk, hv = s.shape
    s5 = s.reshape(b, h, hk // h, h, hv // h)
    return jnp.stack([s5[:, i, :, i, :] for i in range(h)], axis=1)


def _layer(x, wts, layer, depth, kv_stacks, bc, t, chunk, n_valid, tm, s_gla, s_gdn, conv0, ca_fn, diff_fn,
           consts, p64_512):
    lam_init = 0.8 - 0.6 * math.exp(-0.3 * layer)
    gla, q16, k_stack, k16, v_stack, v16, vt16, gdn = _in_proj(
        x, wts["norm_mix"], wts["w_in"], wts["qk_gain"], p64_512, tm, layer, depth, kv_stacks)
    o_gla, s_gla_new = _gla(gla, _expand_state(s_gla, consts["bd_gla"]), wts["gla_w_a2"], wts["gla_b_a"],
                            wts["gla_onorm"], consts, bc, t, chunk, n_valid, min(t, 256))
    o_diff = diff_fn(q16, k16, v16, vt16, wts["lamp"], wts["diff_onorm"], lam_init)
    o_gdn, s_gdn_new, conv_new = _gdn(gdn, _expand_state(s_gdn, consts["bd_gdn"]), conv0, wts["gdn_conv_w"],
                                      wts["gdn_a_log"], wts["gdn_dt_bias"], wts["gdn_onorm"], consts, bc, t,
                                      chunk, n_valid, next(n for n in (4, 2, 1) if bc % n == 0))
    x = _mm_res([o_gla, o_diff, o_gdn], wts["w_out"], x, tm)
    qc = _norm_mm(x, wts["norm_ca"], wts["ca_wq"], wts["ca_qnorm"], tm, headnorm=True, scale=DH_CA ** -0.5,
                  out_dtype=BF16)
    oc = ca_fn(qc)
    x = _mm_res([oc], [wts["ca_wo"]], x, tm)
    hid = _swiglu(x, wts["norm_ffn"], wts["ffn_w1"], wts["ffn_w3"], tm)
    x = _mm_res([hid], [wts["ffn_w2"]], x, tm)
    return (x, (k_stack, v_stack), _compact_state(s_gla_new, H_GLA), _compact_state(s_gdn_new, H_GDN), conv_new)


def _pack_w_in(w):
    sizes = (128, 128, 256, 16, 256, 512, 512, 512, 256, 256, 256, 256, 4, 4)
    segs, off = [], 0
    for s in sizes:
        segs.append(w[:, off:off + s])
        off += s
    gq, gk, gv, ga, gg, dq, dk, dv, nq, nk, nv, nz, na, nb = segs
    z = lambda n: jnp.zeros((w.shape[0], n), w.dtype)
    packed = jnp.concatenate([gq, gk, gv, gg, ga, z(GLA_W - 784), dq, dk, dv, nq, nk, nv, nz, na, nb,
                              z(GDN_W - 1032)], axis=1)
    return packed.astype(BF16)


def _row(v, n=None):
    v = v.reshape(1, -1).astype(F32)
    if n is not None and v.shape[1] < n:
        v = jnp.concatenate([v, jnp.zeros((1, n - v.shape[1]), F32)], axis=1)
    return v


def _layer_weights(l, norm_mix, w_in, gla_w_a2, gla_b_a, gla_onorm, diff_qnorm, diff_knorm, lam_q1, lam_k1,
                   lam_q2, lam_k2, diff_onorm, gdn_conv_w, gdn_a_log, gdn_dt_bias, gdn_onorm, w_out, norm_ca,
                   ca_wq, ca_wo, ca_qnorm, norm_ffn, ffn_w1, ffn_w3, ffn_w2):
    wo = w_out[l].astype(BF16)
    a2 = jnp.concatenate([gla_w_a2[l], jnp.zeros((LANES - GLA_RANK, H_GLA * DK_GLA), F32)], axis=0).astype(BF16)
    n_q = 2 * H_DIFF
    return {
        "norm_mix": _row(norm_mix[l]), "w_in": _pack_w_in(w_in[l]),
        "qk_gain": jnp.concatenate([jnp.tile(_row(diff_qnorm[l]), (1, n_q)) * (DK_DIFF ** -0.5),
                                    jnp.tile(_row(diff_knorm[l]), (1, n_q))], axis=1),
        "gla_w_a2": a2, "gla_b_a": _row(gla_b_a[l]), "gla_onorm": jnp.tile(_row(gla_onorm[l]), (1, H_GLA)),
        "lamp": jnp.stack([lam_q1[l], lam_k1[l], lam_q2[l], lam_k2[l]]).astype(F32),
        "diff_onorm": _row(diff_onorm[l]),
        "gdn_conv_w": gdn_conv_w[l].astype(F32), "gdn_a_log": _row(gdn_a_log[l], LANES),
        "gdn_dt_bias": _row(gdn_dt_bias[l], LANES), "gdn_onorm": jnp.tile(_row(gdn_onorm[l]), (1, H_GDN)),
        "w_out": [wo[0:256], wo[256:768], wo[768:1024]],
        "norm_ca": _row(norm_ca[l]), "ca_wq": ca_wq[l].astype(BF16), "ca_wo": ca_wo[l].astype(BF16),
        "ca_qnorm": _row(ca_qnorm[l]), "norm_ffn": _row(norm_ffn[l]),
        "ffn_w1": ffn_w1[l].astype(BF16), "ffn_w3": ffn_w3[l].astype(BF16), "ffn_w2": ffn_w2[l].astype(BF16),
    }


def _forward(x_prompt, x_sample, mem_prompt, cache_k, cache_v, cache_mem_k, cache_mem_v, state_gla, state_gdn,
             state_conv, page_table, norm_mix, w_in, gla_w_a2, gla_b_a, gla_onorm, diff_qnorm, diff_knorm,
             lam_q1, lam_k1, lam_q2, lam_k2, diff_onorm, gdn_conv_w, gdn_a_log, gdn_dt_bias, gdn_onorm, w_out,
             norm_ca, norm_mem, ca_wq, ca_wk, ca_wv, ca_wo, ca_qnorm, ca_knorm, norm_ffn, ffn_w1, ffn_w3, ffn_w2,
             *, chunk_p, tm_p, tq_p, rows_s, pages):
    depth = w_in.shape[0]
    bp, tp, _ = x_prompt.shape
    bs, ts, _ = x_sample.shape
    n_mem = mem_prompt.shape[1]
    slopes = 2.0 ** (-8.0 * jnp.arange(1, H_DIFF + 1, dtype=F32) / H_DIFF)
    consts_p = _consts(chunk_p)
    consts_s = _consts(rows_s)
    p64_512 = _block_ones(DIFF_W, DK_DIFF).astype(BF16)

    xp = x_prompt.reshape(bp * tp, D_MODEL)
    xs = jnp.pad(x_sample, ((0, 0), (0, rows_s - ts), (0, 0))).reshape(bs * rows_s, D_MODEL)
    memp = mem_prompt.reshape(bp * n_mem, D_MODEL)
    zeros_gla = jnp.zeros((bp, H_GLA, DK_GLA, DV_GLA), F32)
    zeros_gdn = jnp.zeros((bp, H_GDN, DK_GDN, DV_GDN), F32)
    zeros_conv = jnp.zeros((bp, CONV_W - 1, C_CONV), F32)
    tm_m = min(bp * n_mem, 512)
    tm_s = bs * rows_s
    cache_k = cache_k.reshape(cache_k.shape[0], cache_k.shape[1], PAGE_SIZE * H_DIFF, LANES)
    cache_v = cache_v.reshape(cache_v.shape[0], cache_v.shape[1], PAGE_SIZE * H_DIFF, LANES)

    mem_k_s = cache_mem_k.reshape(depth, bs, n_mem * H_CA, DH_CA)
    mem_v_s = cache_mem_v.reshape(depth, bs, n_mem * H_CA, DH_CA)
    kv_p = tuple(jnp.zeros((depth, bp * tp * H_DIFF, LANES), F32) for _ in range(2))
    kv_s = tuple(jnp.zeros((depth, bs * rows_s * H_DIFF, LANES), F32) for _ in range(2))

    outs = [[] for _ in range(8)]
    for l in range(depth):
        wts = _layer_weights(l, norm_mix, w_in, gla_w_a2, gla_b_a, gla_onorm, diff_qnorm, diff_knorm, lam_q1,
                             lam_k1, lam_q2, lam_k2, diff_onorm, gdn_conv_w, gdn_a_log, gdn_dt_bias, gdn_onorm,
                             w_out, norm_ca, ca_wq, ca_wo, ca_qnorm, norm_ffn, ffn_w1, ffn_w3, ffn_w2)
        mk, mk16 = _norm_mm(memp, _row(norm_mem[l]), ca_wk[l].astype(BF16), _row(ca_knorm[l]), tm_m, headnorm=True,
                            scale=1.0, out_dtype=F32, head_major_copy=True)
        mv, mv16 = _norm_mm(memp, _row(norm_mem[l]), ca_wv[l].astype(BF16), _row(ca_knorm[l]), tm_m, headnorm=False,
                            scale=1.0, out_dtype=F32, head_major_copy=True)

        def diff_p(q16, k16, v16, vt16, lamp, onorm, lam_init):
            return _diff_prompt(q16, k16, vt16, slopes, lamp, onorm, bp, tp, tq_p, lam_init)

        def ca_p(qc, mk16=mk16, mv16=mv16):
            spec = pl.BlockSpec((H_CA, n_mem, DH_CA), lambda b, i: (0, b, 0))
            return _ca_attn(qc, mk16, mv16, bp, tp, min(tp, 512), spec, False)

        xp, kv_p, sgp, sdp, cvp = _layer(xp, wts, l, depth, kv_p, bp, tp, chunk_p, chunk_p, tm_p, zeros_gla,
                                         zeros_gdn, zeros_conv, ca_p, diff_p, consts_p, p64_512)

        def diff_s(q16, k16, v16, vt16, lamp, onorm, lam_init, l=l):
            return _diff_decode(q16, k16, v16, cache_k, cache_v, page_table, slopes, lamp, onorm, l, bs, rows_s,
                                ts, lam_init, pages)

        def ca_s(qc, l=l):
            spec = pl.BlockSpec((None, None, n_mem * H_CA, DH_CA), lambda b, i: (l, b, 0, 0))
            return _ca_attn(qc, mem_k_s, mem_v_s, bs, rows_s, rows_s, spec, True)

        xs, kv_s, sgs, sds, cvs = _layer(xs, wts, l, depth, kv_s, bs, rows_s, rows_s, ts, tm_s, state_gla[l],
                                         state_gdn[l], state_conv[l], ca_s, diff_s, consts_s, p64_512)
        vals = (sgp, sgs, sdp, sds, cvp, cvs, mk.reshape(bp, n_mem, H_CA, DH_CA), mv.reshape(bp, n_mem, H_CA, DH_CA))
        for o, v in zip(outs, vals):
            o.append(v)

    y_p = xp.reshape(bp, tp, D_MODEL)
    y_s = xs.reshape(bs, rows_s, D_MODEL)[:, :ts]
    kv_out = (kv_p[0].reshape(depth, bp, tp, H_DIFF, 2 * DK_DIFF), kv_p[1].reshape(depth, bp, tp, H_DIFF, DV_DIFF),
              kv_s[0].reshape(depth, bs, rows_s, H_DIFF, 2 * DK_DIFF)[:, :, :ts],
              kv_s[1].reshape(depth, bs, rows_s, H_DIFF, DV_DIFF)[:, :, :ts])
    return (y_p, y_s) + kv_out + tuple(jnp.stack(o) for o in outs)


def kernel(x_prompt, x_sample, mem_prompt, cache_k, cache_v, cache_mem_k, cache_mem_v, state_gla, state_gdn, state_conv, page_table, norm_mix, w_in, gla_w_a2, gla_b_a, gla_onorm, diff_qnorm, diff_knorm, lam_q1, lam_k1, lam_q2, lam_k2, diff_onorm, gdn_conv_w, gdn_a_log, gdn_dt_bias, gdn_onorm, w_out, norm_ca, norm_mem, ca_wq, ca_wk, ca_wv, ca_wo, ca_qnorm, ca_knorm, norm_ffn, ffn_w1, ffn_w3, ffn_w2):
    return _forward(x_prompt, x_sample, mem_prompt, cache_k, cache_v, cache_mem_k, cache_mem_v, state_gla,
                    state_gdn, state_conv, page_table, norm_mix, w_in, gla_w_a2, gla_b_a, gla_onorm, diff_qnorm,
                    diff_knorm, lam_q1, lam_k1, lam_q2, lam_k2, diff_onorm, gdn_conv_w, gdn_a_log, gdn_dt_bias,
                    gdn_onorm, w_out, norm_ca, norm_mem, ca_wq, ca_wk, ca_wv, ca_wo, ca_qnorm, ca_knorm, norm_ffn,
                    ffn_w1, ffn_w3, ffn_w2, chunk_p=64, tm_p=512, tq_p=1024, rows_s=16, pages=16)
```

```python
import functools
import math

import jax
import jax.numpy as jnp
from jax import lax
from jax.experimental import pallas as pl
from jax.experimental.pallas import tpu as pltpu

F32 = jnp.float32
BF16 = jnp.bfloat16

D_MODEL = 1024
DEPTH = 4
PAGE_SIZE = 128
H_GLA, DK_GLA, DV_GLA, GLA_RANK, GLA_TAU = 4, 32, 64, 16, 16.0
H_DIFF, DK_DIFF, DV_DIFF = 4, 64, 128
H_GDN, DK_GDN, DV_GDN, CONV_W = 4, 64, 64, 4
C_CONV = 2 * H_GDN * DK_GDN + H_GDN * DV_GDN
N_MEM, H_CA = 256, 4
DH_CA = D_MODEL // H_CA
D_FF = -(-8 * D_MODEL // (3 * 256)) * 256
RMS_EPS = 1e-6
L2_EPS = 1e-6

VMEM_LIMIT_V7X = 56 * 1024 * 1024
LANES = 128

GLA_W = 896
GDN_W = 1152
DIFF_W = 512
OFF_GLA, OFF_DQ, OFF_DK, OFF_DV, OFF_GDN = 0, 896, 1408, 1920, 2432
IN_W = OFF_GDN + GDN_W

NEG_BIG = -1e30


def _dot(a, b):
    return jnp.dot(a, b, preferred_element_type=F32)


def _dot_nt(a, b):
    return lax.dot_general(a, b, (((1,), (1,)), ((), ())), preferred_element_type=F32)


def _hilo(a):
    hi = a.astype(BF16)
    lo = (a - hi.astype(F32)).astype(BF16)
    return hi, lo


def _dot_sel_rhs(a, sel):
    hi, lo = _hilo(a)
    return _dot(hi, sel) + _dot(lo, sel)


def _dot_sel_lhs(sel, b):
    hi, lo = _hilo(b)
    return _dot(sel, hi) + _dot(sel, lo)


def _dot3(a, b):
    ah, al = _hilo(a)
    bh, bl = _hilo(b)
    return _dot(ah, bh) + _dot(ah, bl) + _dot(al, bh)


def _sigmoid(x):
    return 1.0 / (1.0 + jnp.exp(-x))


def _softplus(x):
    return jnp.maximum(x, 0.0) + jnp.log(1.0 + jnp.exp(-jnp.abs(x)))


def _rms_rows(x, g):
    ms = jnp.mean(x * x, axis=-1, keepdims=True)
    return x * lax.rsqrt(ms + RMS_EPS) * g


def _pad_rows_128(a):
    r = a.shape[0]
    if r == LANES:
        return a
    return jnp.concatenate([a, jnp.zeros((LANES - r, a.shape[1]), a.dtype)], axis=0)


def _const_spec(shape):
    nd = len(shape)
    return pl.BlockSpec(shape, lambda *_: (0,) * nd)


def _params(sem):
    return pltpu.CompilerParams(dimension_semantics=sem, vmem_limit_bytes=VMEM_LIMIT_V7X)


def _in_proj_kernel(x_ref, g_ref, w_ref, qkg_ref, p64_ref, *rest):
    gla_ref, q_ref, k_ref, kb_ref, v_ref, vb_ref, vt_ref, gdn_ref = rest[-8:]
    xn = _rms_rows(x_ref[...], g_ref[...]).astype(BF16)
    gla_ref[...] = _dot(xn, w_ref[:, OFF_GLA:OFF_GLA + GLA_W])

    def qknorm(y, gain):
        ss = _dot_sel_rhs(y * y, p64_ref[...])
        return y * lax.rsqrt(ss * (1.0 / DK_DIFF) + RMS_EPS) * gain

    yq = _dot(xn, w_ref[:, OFF_DQ:OFF_DQ + DIFF_W])
    q_ref[...] = qknorm(yq, qkg_ref[:, 0:DIFF_W]).astype(BF16)
    yk = _dot(xn, w_ref[:, OFF_DK:OFF_DK + DIFF_W])
    kk = qknorm(yk, qkg_ref[:, DIFF_W:2 * DIFF_W])
    kb_ref[...] = kk.astype(BF16)
    yv = _dot(xn, w_ref[:, OFF_DV:OFF_DV + DIFF_W])
    vb_ref[...] = yv.astype(BF16)
    tm = kk.shape[0]
    for h in range(H_DIFF):
        k_ref[pl.ds(h, tm, stride=H_DIFF), :] = kk[:, h * LANES:(h + 1) * LANES]
        v_ref[pl.ds(h, tm, stride=H_DIFF), :] = yv[:, h * LANES:(h + 1) * LANES]
    vt_ref[...] = jnp.transpose(yv).astype(BF16)
    gdn_ref[...] = _dot(xn, w_ref[:, OFF_GDN:OFF_GDN + GDN_W])


def _in_proj(x, g, w, qkg, p64, tm, layer, depth, stacks):
    m = x.shape[0]
    row = lambda wd: pl.BlockSpec((tm, wd), lambda i: (i, 0))
    slot = pl.BlockSpec((None, tm * H_DIFF, LANES), lambda i: (layer, i, 0))
    stack_shape = jax.ShapeDtypeStruct((depth, m * H_DIFF, LANES), F32)
    in_specs = [row(D_MODEL), _const_spec((1, D_MODEL)), _const_spec((D_MODEL, IN_W)),
                _const_spec((1, 2 * DIFF_W)), _const_spec((DIFF_W, DIFF_W))]
    args = [x, g, w, qkg, p64]
    aliases = {}
    if stacks is not None:
        in_specs += [pl.BlockSpec(memory_space=pl.ANY), pl.BlockSpec(memory_space=pl.ANY)]
        args += list(stacks)
        aliases = {5: 2, 6: 4}
    return pl.pallas_call(
        _in_proj_kernel,
        grid=(m // tm,),
        in_specs=in_specs,
        out_specs=[row(GLA_W), row(DIFF_W), slot, row(DIFF_W), slot, row(DIFF_W),
                   pl.BlockSpec((DIFF_W, tm), lambda i: (0, i)), row(GDN_W)],
        out_shape=[jax.ShapeDtypeStruct((m, GLA_W), F32), jax.ShapeDtypeStruct((m, DIFF_W), BF16),
                   stack_shape, jax.ShapeDtypeStruct((m, DIFF_W), BF16),
                   stack_shape, jax.ShapeDtypeStruct((m, DIFF_W), BF16),
                   jax.ShapeDtypeStruct((DIFF_W, m), BF16), jax.ShapeDtypeStruct((m, GDN_W), F32)],
        input_output_aliases=aliases,
        compiler_params=_params(("parallel",)),
        name="in_proj",
    )(*args)


def _norm_mm_kernel(x_ref, g_ref, w_ref, hg_ref, o_ref, *maybe_heads_ref, headnorm, scale):
    xn = _rms_rows(x_ref[...], g_ref[...]).astype(BF16)
    y = _dot(xn, w_ref[...])
    for h in range(H_CA):
        yh = y[:, h * DH_CA:(h + 1) * DH_CA]
        if headnorm:
            yh = _rms_rows(yh, hg_ref[...]) * scale
        o_ref[:, h * DH_CA:(h + 1) * DH_CA] = yh.astype(o_ref.dtype)
        if maybe_heads_ref:
            maybe_heads_ref[0][h] = yh.astype(BF16)


def _norm_mm(x, g, w, hg, tm, *, headnorm, scale, out_dtype, head_major_copy=False):
    m, n = x.shape[0], w.shape[1]
    out_specs = [pl.BlockSpec((tm, n), lambda i: (i, 0))]
    out_shape = [jax.ShapeDtypeStruct((m, n), out_dtype)]
    if head_major_copy:
        out_specs.append(pl.BlockSpec((H_CA, tm, DH_CA), lambda i: (0, i, 0)))
        out_shape.append(jax.ShapeDtypeStruct((H_CA, m, DH_CA), BF16))
    outs = pl.pallas_call(
        functools.partial(_norm_mm_kernel, headnorm=headnorm, scale=scale),
        grid=(m // tm,),
        in_specs=[pl.BlockSpec((tm, D_MODEL), lambda i: (i, 0)), _const_spec((1, D_MODEL)),
                  _const_spec((D_MODEL, n)), _const_spec((1, DH_CA))],
        out_specs=out_specs,
        out_shape=out_shape,
        compiler_params=_params(("parallel",)),
        name="norm_mm",
    )(x, g, w, hg)
    return outs if head_major_copy else outs[0]


def _mm_res_kernel(*refs, n):
    xs, ws, res_ref, o_ref = refs[:n], refs[n:2 * n], refs[2 * n], refs[2 * n + 1]
    acc = res_ref[...]
    for x_ref, w_ref in zip(xs, ws):
        acc = acc + _dot(x_ref[...], w_ref[...])
    o_ref[...] = acc


def _mm_res(xs, ws, res, tm):
    m, n = res.shape
    k = len(xs)
    in_specs = ([pl.BlockSpec((tm, x.shape[1]), lambda i: (i, 0)) for x in xs]
                + [_const_spec(w.shape) for w in ws]
                + [pl.BlockSpec((tm, n), lambda i: (i, 0))])
    return pl.pallas_call(
        functools.partial(_mm_res_kernel, n=k),
        grid=(m // tm,),
        in_specs=in_specs,
        out_specs=pl.BlockSpec((tm, n), lambda i: (i, 0)),
        out_shape=jax.ShapeDtypeStruct((m, n), F32),
        compiler_params=_params(("parallel",)),
        name="mm_res",
    )(*xs, *ws, res)


FF_CHUNK = D_FF // 2


def _swiglu_kernel(x_ref, g_ref, w1_ref, w3_ref, o_ref):
    xn = _rms_rows(x_ref[...], g_ref[...]).astype(BF16)
    for c in range(D_FF // FF_CHUNK):
        sl = slice(c * FF_CHUNK, (c + 1) * FF_CHUNK)
        a = _dot(xn, w1_ref[:, sl])
        b = _dot(xn, w3_ref[:, sl])
        o_ref[:, sl] = (a * _sigmoid(a) * b).astype(BF16)


def _swiglu(x, g, w1, w3, tm):
    m = x.shape[0]
    return pl.pallas_call(
        _swiglu_kernel,
        grid=(m // tm,),
        in_specs=[pl.BlockSpec((tm, D_MODEL), lambda i: (i, 0)), _const_spec((1, D_MODEL)),
                  _const_spec((D_MODEL, D_FF)), _const_spec((D_MODEL, D_FF))],
        out_specs=pl.BlockSpec((tm, D_FF), lambda i: (i, 0)),
        out_shape=jax.ShapeDtypeStruct((m, D_FF), BF16),
        compiler_params=_params(("parallel",)),
        name="swiglu",
    )(x, g, w1, w3)


GLA_SB = 16


def _gla_chunk(p, s, wa_ref, ba_ref, on_ref, cum_ref, seg_ref, p64_ref, bd_ref, b_s, q_s, k_s, v_s, tall_s,
               *, chunk, n_valid):
    hk, hv = H_GLA * DK_GLA, H_GLA * DV_GLA
    nsb = chunk // GLA_SB
    q = p[:, 0:128] * (DK_GLA ** -0.5)
    k = p[:, 128:256]
    v = p[:, 256:512]
    gate = p[:, 512:768]
    x = _dot(p[:, 768:896].astype(BF16), wa_ref[...]) + ba_ref[...]
    yield
    la = (jnp.minimum(x, 0.0) - jnp.log(1.0 + jnp.exp(-jnp.abs(x)))) * (1.0 / GLA_TAU)
    if n_valid < chunk:
        rowv = lax.broadcasted_iota(jnp.int32, (chunk, 1), 0) < n_valid
        la = jnp.where(rowv, la, 0.0)
        k = jnp.where(rowv, k, 0.0)
    cums = _dot_sel_lhs(cum_ref[...], la)
    yield
    b = cums[0:chunk]
    b_s[...] = b
    q_s[...] = q
    k_s[...] = k
    v_s[...] = v
    v16 = v.astype(BF16)
    o = _dot((q * jnp.exp(b)).astype(BF16), s.astype(BF16))
    yield

    if nsb > 1:
        mq, mqc = cums[chunk:2 * chunk], cums[2 * chunk:3 * chunk]
        mh, mhc = cums[3 * chunk:4 * chunk], cums[4 * chunk:5 * chunk]
        lane = lax.broadcasted_iota(jnp.int32, (1, hk), 1)

        def stacked(a):
            return jnp.concatenate(
                [jnp.where((lane >= h * DK_GLA) & (lane < (h + 1) * DK_GLA), a, 0.0) for h in range(H_GLA)],
                axis=0).astype(BF16)

        a2 = _dot_nt(stacked(q * jnp.exp(b - mq)), (k * jnp.exp(mqc - b)).astype(BF16))
        a1 = _dot_nt(stacked(q * jnp.exp(b - mh)), (k * jnp.exp(mhc - b)).astype(BF16))
        yield
        qi =(lax.broadcasted_iota(jnp.int32, (H_GLA * chunk, chunk), 0) & (chunk - 1)) // GLA_SB
        qj = lax.broadcasted_iota(jnp.int32, (H_GLA * chunk, chunk), 1) // GLA_SB
        att = jnp.where(qi == qj + 1, a2, jnp.where(qi > qj + 1, a1, 0.0))
        r = _dot(att.astype(BF16), v16)
        yield
        lane_v = lax.broadcasted_iota(jnp.int32, (1, hv), 1)
        for h in range(H_GLA):
            o = o + jnp.where((lane_v >= h * DV_GLA) & (lane_v < (h + 1) * DV_GLA),
                              r[h * chunk:(h + 1) * chunk], 0.0)

    row = lax.broadcasted_iota(jnp.int32, (GLA_SB, 1), 0)
    for j in range(GLA_SB):
        for d in range(nsb):
            sl = slice(d * GLA_SB, (d + 1) * GLA_SB)
            bj = b_s[d * GLA_SB + j:d * GLA_SB + j + 1, :]
            kj = k_s[d * GLA_SB + j:d * GLA_SB + j + 1, :]
            e = jnp.exp(jnp.where(row >= j, b_s[sl, :] - bj, -jnp.inf))
            tall_s[j * chunk + d * GLA_SB:j * chunk + (d + 1) * GLA_SB, :] = (q_s[sl, :] * kj * e).astype(BF16)
    tsum = _dot(tall_s[...], seg_ref[...])
    yield
    for j in range(GLA_SB):
        vj = jnp.concatenate(
            [jnp.broadcast_to(v_s[d * GLA_SB + j:d * GLA_SB + j + 1, :], (GLA_SB, hv)) for d in range(nsb)], axis=0)
        o = o + tsum[j * chunk:(j + 1) * chunk] * vj

    b_end = b[chunk - 1:chunk, :]
    kd = k * jnp.exp(b_end - b)
    kd_t = jnp.transpose(_pad_rows_128(kd)).astype(BF16)
    upd = _dot(kd_t, _pad_rows_128(v).astype(BF16))
    yield
    dcol = jnp.transpose(jnp.broadcast_to(jnp.exp(b_end), (LANES, LANES)))[:, 0:1]
    s_new = s * dcol + upd * bd_ref[...]

    ss = _dot_sel_rhs(o * o, p64_ref[...])
    on = o * lax.rsqrt(ss * (1.0 / DV_GLA) + RMS_EPS) * on_ref[...]
    return on * gate * _sigmoid(gate), s_new


def _run_stagewise(gens):
    results = [None] * len(gens)
    live = list(range(len(gens)))
    while live:
        for i in list(live):
            try:
                next(gens[i])
            except StopIteration as stop:
                results[i] = stop.value
                live.remove(i)
    return results


def _gla_kernel(p_ref, s0_ref, wa_ref, ba_ref, on_ref, cum_ref, seg_ref, p64_ref, bd_ref,
                o_ref, sout_ref, s_s, b_s, q_s, k_s, v_s, tall_s, *, chunk, n_chunks, n_valid, nseq):
    c = pl.program_id(1)

    @pl.when(c == 0)
    def _():
        s_s[...] = s0_ref[...]

    def body(ci, states):
        r0 = pl.multiple_of(ci * chunk, chunk)
        gens = [_gla_chunk(p_ref[n, pl.ds(r0, chunk), :], states[n], wa_ref, ba_ref, on_ref, cum_ref, seg_ref,
                           p64_ref, bd_ref, b_s.at[n], q_s.at[n], k_s.at[n], v_s.at[n], tall_s.at[n],
                           chunk=chunk, n_valid=n_valid) for n in range(nseq)]
        new_states = []
        for n, (o, s_new) in enumerate(_run_stagewise(gens)):
            o_ref[n, pl.ds(r0, chunk), :] = o.astype(BF16)
            new_states.append(s_new)
        return tuple(new_states)

    states = lax.fori_loop(0, n_chunks, body, tuple(s_s[n] for n in range(nseq)))
    for n in range(nseq):
        s_s[n] = states[n]

    @pl.when(c == pl.num_programs(1) - 1)
    def _():
        for n in range(nseq):
            sout_ref[n] = states[n]


def _gla(proj, s0, wa, ba, onorm, consts, bc, t, chunk, n_valid, block, nseq):
    nb = t // block
    hk, hv = H_GLA * DK_GLA, H_GLA * DV_GLA
    cum = consts["gla_cum"]
    assert bc % nseq == 0
    o, s_out = pl.pallas_call(
        functools.partial(_gla_kernel, chunk=chunk, n_chunks=block // chunk, n_valid=n_valid, nseq=nseq),
        grid=(bc // nseq, nb),
        in_specs=[pl.BlockSpec((nseq, block, GLA_W), lambda b, c: (b, c, 0)),
                  pl.BlockSpec((nseq, hk, hv), lambda b, c: (b, 0, 0)),
                  _const_spec((LANES, hk)), _const_spec((1, hk)), _const_spec((1, hv)),
                  _const_spec(cum.shape), _const_spec((hk, hv)), _const_spec((hv, hv)),
                  _const_spec((hk, hv))],
        out_specs=[pl.BlockSpec((nseq, block, hv), lambda b, c: (b, c, 0)),
                   pl.BlockSpec((nseq, hk, hv), lambda b, c: (b, 0, 0))],
        out_shape=[jax.ShapeDtypeStruct((bc, t, hv), BF16), jax.ShapeDtypeStruct((bc, hk, hv), F32)],
        scratch_shapes=[pltpu.VMEM((nseq, hk, hv), F32), pltpu.VMEM((nseq, chunk, hk), F32),
                        pltpu.VMEM((nseq, chunk, hk), F32), pltpu.VMEM((nseq, chunk, hk), F32),
                        pltpu.VMEM((nseq, chunk, hv), F32), pltpu.VMEM((nseq, GLA_SB * chunk, hk), BF16)],
        compiler_params=_params(("parallel", "arbitrary")),
        name="gla",
    )(proj.reshape(bc, t, GLA_W), s0, wa, ba, onorm, cum, consts["seg_gla"], consts["p64_256"], consts["bd_gla"])
    return o.reshape(bc * t, hv), s_out


def _gdn_kernel(p_ref, s0_ref, cb_ref, cw_ref, alog_ref, dtb_ref, on_ref,
                tri_ref, p64_ref, eg_ref, eg64_ref, eb64_ref, bd_ref,
                o_ref, sout_ref, cout_ref, s_s, xp_s, *, chunk, n_valid, nseq):
    c = pl.program_id(1)
    hd = H_GDN * DK_GDN
    last = c == pl.num_programs(1) - 1

    @pl.when(c == 0)
    def _():
        s_s[...] = s0_ref[...]
        xp_s[:, 5:8, :] = cb_ref[...]

    def l2n(a):
        ss = _dot_sel_rhs(a * a, p64_ref[...])
        return a * lax.rsqrt(ss + L2_EPS)

    ri = lax.broadcasted_iota(jnp.int32, (chunk, chunk), 0)
    ci = lax.broadcasted_iota(jnp.int32, (chunk, chunk), 1)
    eye = (ri == ci).astype(F32)
    lane = lax.broadcasted_iota(jnp.int32, (1, hd), 1)
    lms = [(lane >= h * DK_GDN) & (lane < (h + 1) * DK_GDN) for h in range(H_GDN)]
    n_sq = int(math.log2(chunk)) - 1

    pre = []
    for n in range(nseq):
        p = p_ref[n]
        xp_s[n, 8:8 + chunk, :] = p[:, 0:C_CONV]
        y = xp_s[n, pl.ds(5, chunk), :] * cw_ref[0:1, :]
        for i in range(1, CONV_W):
            y = y + xp_s[n, pl.ds(5 + i, chunk), :] * cw_ref[i:i + 1, :]
        y = y * _sigmoid(y)
        tail = xp_s[n, pl.ds(5 + n_valid, CONV_W - 1), :]
        xp_s[n, 5:8, :] = tail

        @pl.when(last)
        def _(n=n, tail=tail):
            cout_ref[n] = tail

        cq = l2n(y[:, 0:hd]) * (DK_GDN ** -0.5)
        ck = l2n(y[:, hd:2 * hd])
        cv = y[:, 2 * hd:3 * hd]
        small = p[:, 1024:1152]
        g = -jnp.exp(alog_ref[...]) * _softplus(small + dtb_ref[...])
        beta = _sigmoid(small)
        if n_valid < chunk:
            rowv = lax.broadcasted_iota(jnp.int32, (chunk, 1), 0) < n_valid
            g = jnp.where(rowv, g, 0.0)
            beta = jnp.where(rowv, beta, 0.0)
        gam = _dot_sel_lhs(tri_ref[...], g)
        gam_w = _dot_sel_rhs(gam, eg_ref[...])
        gam_b = _dot_sel_rhs(gam, eg64_ref[...])
        beta_b = _dot_sel_rhs(beta, eb64_ref[...])
        gam_t = jnp.transpose(_pad_rows_128(gam))
        eg = jnp.exp(gam_b)
        kb = ck * beta_b
        g_end = gam_b[chunk - 1:chunk, :]
        s = s_s[n]
        pre.append(dict(
            z=p[:, 768:1024], cq=cq, ck16=ck.astype(BF16), eg=eg, kb=kb, kbg=kb * eg, vb=cv * beta_b, g_end=g_end,
            kdec=ck * jnp.exp(g_end - gam_b), s=s, s16=s.astype(BF16),
            decs=[jnp.exp(jnp.where(ri >= ci, gam_w[:, h * LANES:h * LANES + chunk] - gam_t[h:h + 1, 0:chunk],
                                    -jnp.inf)) for h in range(H_GDN)]))

    pairs = [(n, h) for n in range(nseq) for h in range(H_GDN)]
    pws = [-(_dot_nt(jnp.where(lms[h], pre[n]["kb"], 0.0).astype(BF16), pre[n]["ck16"])
             * jnp.where(ri > ci, pre[n]["decs"][h], 0.0)) for n, h in pairs]
    atts = [(_dot_nt(jnp.where(lms[h], pre[n]["cq"], 0.0).astype(BF16), pre[n]["ck16"])
             * pre[n]["decs"][h]).astype(BF16) for n, h in pairs]
    tinvs = [eye + pw for pw in pws]
    for _ in range(n_sq):
        pws = [_dot3(pw, pw) for pw in pws]
        tinvs = [ti + _dot3(ti, pw) for ti, pw in zip(tinvs, pws)]
    sols = [_dot3(ti, jnp.concatenate([jnp.where(lms[h], pre[n]["vb"], 0.0), jnp.where(lms[h], pre[n]["kbg"], 0.0)],
                                      axis=1)) for ti, (n, h) in zip(tinvs, pairs)]

    def seq_sum(n, lo):
        parts = [sols[n * H_GDN + h][:, lo:lo + hd] for h in range(H_GDN)]
        return (parts[0] + parts[1]) + (parts[2] + parts[3])

    v_news = [seq_sum(n, 0) - _dot(seq_sum(n, hd).astype(BF16), pre[n]["s16"]) for n in range(nseq)]
    outs = [_dot((pre[n]["cq"] * pre[n]["eg"]).astype(BF16), pre[n]["s16"]) for n in range(nseq)]
    intra = [_dot(atts[i], jnp.where(lms[h], v_news[n], 0.0).astype(BF16)) for i, (n, h) in enumerate(pairs)]
    upds = [_dot(jnp.transpose(_pad_rows_128(pre[n]["kdec"])).astype(BF16), _pad_rows_128(v_news[n]).astype(BF16))
            for n in range(nseq)]
    for n in range(nseq):
        o = outs[n]
        for h in range(H_GDN):
            o = o + intra[n * H_GDN + h]
        s_new = pre[n]["s"] * jnp.exp(pre[n]["g_end"]) + upds[n] * bd_ref[...]
        s_s[n] = s_new
        ss = _dot_sel_rhs(o * o, p64_ref[...])
        on = o * lax.rsqrt(ss * (1.0 / DV_GDN) + RMS_EPS) * on_ref[...]
        z = pre[n]["z"]
        o_ref[n] = (on * z * _sigmoid(z)).astype(BF16)

        @pl.when(last)
        def _(n=n, s_new=s_new):
            sout_ref[n] = s_new


def _gdn(proj, s0, conv0, cw, alog, dtb, onorm, consts, bc, t, chunk, n_valid, nseq):
    nc = t // chunk
    hd = H_GDN * DK_GDN
    assert bc % nseq == 0
    o, s_out, conv_out = pl.pallas_call(
        functools.partial(_gdn_kernel, chunk=chunk, n_valid=n_valid, nseq=nseq),
        grid=(bc // nseq, nc),
        in_specs=[pl.BlockSpec((nseq, chunk, GDN_W), lambda b, c: (b, c, 0)),
                  pl.BlockSpec((nseq, hd, hd), lambda b, c: (b, 0, 0)),
                  pl.BlockSpec((nseq, CONV_W - 1, C_CONV), lambda b, c: (b, 0, 0)),
                  _const_spec((CONV_W, C_CONV)), _const_spec((1, LANES)), _const_spec((1, LANES)),
                  _const_spec((1, hd)),
                  _const_spec((chunk, chunk)), _const_spec((hd, hd)), _const_spec((LANES, 4 * LANES)),
                  _const_spec((LANES, hd)), _const_spec((LANES, hd)), _const_spec((hd, hd))],
        out_specs=[pl.BlockSpec((nseq, chunk, hd), lambda b, c: (b, c, 0)),
                   pl.BlockSpec((nseq, hd, hd), lambda b, c: (b, 0, 0)),
                   pl.BlockSpec((nseq, CONV_W - 1, C_CONV), lambda b, c: (b, 0, 0))],
        out_shape=[jax.ShapeDtypeStruct((bc, t, hd), BF16), jax.ShapeDtypeStruct((bc, hd, hd), F32),
                   jax.ShapeDtypeStruct((bc, CONV_W - 1, C_CONV), F32)],
        scratch_shapes=[pltpu.VMEM((nseq, hd, hd), F32), pltpu.VMEM((nseq, chunk + 8, C_CONV), F32)],
        compiler_params=_params(("parallel", "arbitrary")),
        name="gdn",
    )(proj.reshape(bc, t, GDN_W), s0, conv0, cw, alog, dtb, onorm, consts["tri"], consts["p64_256"], consts["eg"],
      consts["eg64"], consts["eb64"], consts["bd_gdn"])
    return o.reshape(bc * t, hd), s_out, conv_out


def _lam_value(lamp_ref, lam_init):
    lp = lamp_ref[...]
    return (jnp.exp(jnp.sum(lp[0:1] * lp[1:2], axis=-1, keepdims=True))
            - jnp.exp(jnp.sum(lp[2:3] * lp[3:4], axis=-1, keepdims=True)) + lam_init)


def _softmax_step(carry, s, v16):
    m, l, acc = carry
    m_new = jnp.maximum(m, jnp.max(s, axis=-1, keepdims=True))
    alpha = jnp.exp(m - m_new)
    p = jnp.exp(s - m_new)
    l = alpha * l + jnp.sum(p, axis=-1, keepdims=True)
    acc = alpha * acc + _dot(p.astype(BF16), v16)
    return m_new, l, acc


def _diff_prompt_kernel(slope_ref, lamp_ref, ong_ref, q_ref, k_ref, vt_ref, o_ref, *, tq, lam_init):
    h = pl.program_id(1)
    qi = pl.program_id(2)
    w = 2 * tq
    slope = slope_ref[h]
    lam = _lam_value(lamp_ref, lam_init)
    krow = lax.broadcasted_iota(jnp.int32, (tq, w), 0)
    base = slope * krow.astype(F32)
    rowc = lax.broadcasted_iota(jnp.int32, (LANES, tq), 0)
    q_t = jnp.transpose(q_ref[...].astype(F32))
    qt = jnp.concatenate([jnp.where(rowc < DK_DIFF, q_t, 0.0), jnp.where(rowc >= DK_DIFF, q_t, 0.0)],
                         axis=1).astype(BF16)

    def block(kj, carry, masked):
        m, l, acc = carry
        off = pl.multiple_of(kj * tq, tq)
        s = _dot(k_ref[pl.ds(off, tq), :], qt) + base
        if masked:
            qpos = lax.broadcasted_iota(jnp.int32, (tq, w), 1) & (tq - 1)
            s = jnp.where(krow <= qpos, s, NEG_BIG)
        c = slope * ((kj - qi) * tq).astype(F32)
        m_new = jnp.maximum(m, jnp.max(s, axis=0, keepdims=True) + c)
        p = jnp.exp(s - (m_new - c))
        alpha = jnp.exp(m - m_new)
        l = alpha * l + jnp.sum(p, axis=0, keepdims=True)
        acc = alpha * acc + _dot(vt_ref[:, pl.ds(off, tq)], p.astype(BF16))
        return m_new, l, acc

    carry = (jnp.full((1, w), NEG_BIG, F32), jnp.zeros((1, w), F32), jnp.zeros((LANES, w), F32))
    carry = lax.fori_loop(0, qi, functools.partial(block, masked=False), carry)
    _, l, acc = block(qi, carry, True)
    o = acc / l
    d = o[:, 0:tq] - lam * o[:, tq:w]
    ms = jnp.mean(d * d, axis=0, keepdims=True)
    dn = d * lax.rsqrt(ms + RMS_EPS) * ong_ref[...] * (1.0 - lam_init)
    o_ref[...] = jnp.transpose(dn).astype(BF16)


def _diff_prompt(q, k16, vt16, slopes, lamp, onorm, bc, t, tq, lam_init):
    nq = t // tq
    assert tq & (tq - 1) == 0 and DV_DIFF == LANES
    ong = jnp.broadcast_to(onorm.reshape(DV_DIFF, 1), (DV_DIFF, tq))
    return pl.pallas_call(
        functools.partial(_diff_prompt_kernel, tq=tq, lam_init=lam_init),
        grid=(bc, H_DIFF, nq),
        in_specs=[pl.BlockSpec(memory_space=pltpu.SMEM), _const_spec((4, DK_DIFF)), _const_spec((DV_DIFF, tq)),
                  pl.BlockSpec((tq, LANES), lambda b, h, i: (b * nq + i, h)),
                  pl.BlockSpec((t, LANES), lambda b, h, i: (b, h)),
                  pl.BlockSpec((LANES, t), lambda b, h, i: (h, b))],
        out_specs=pl.BlockSpec((tq, LANES), lambda b, h, i: (b * nq + i, h)),
        out_shape=jax.ShapeDtypeStruct((bc * t, H_DIFF * DV_DIFF), BF16),
        compiler_params=_params(("parallel", "parallel", "arbitrary")),
        name="diff_prompt",
    )(slopes, lamp, ong, q, k16, vt16)


DEC_GROUP = 4


def _diff_decode_kernel(pt_ref, lamp_ref, on_ref, scol_ref, srow_ref, q_ref, kn_ref, vn_ref, *rest,
                        pages, rows, n_valid, past_len, lam_init):
    k_refs, v_refs = rest[:pages], rest[pages:2 * pages]
    o_ref, qq_s, m_s, l_s, acc_s = rest[2 * pages:]
    del pt_ref
    g = pl.program_id(1)
    qr = 2 * n_valid
    nr = H_DIFF * qr
    prow = PAGE_SIZE * H_DIFF
    rowh = lax.broadcasted_iota(jnp.int32, (nr, 1), 0) // qr

    @pl.when(g == 0)
    def _():
        row = lax.broadcasted_iota(jnp.int32, (rows, 1), 0)
        lane = lax.broadcasted_iota(jnp.int32, (rows, LANES), 1)
        for h in range(H_DIFF):
            q = q_ref[:, h * LANES:(h + 1) * LANES].astype(F32)
            q1 = jnp.where(lane < DK_DIFF, q, 0.0)
            q2 = pltpu.roll(jnp.where(lane >= DK_DIFF, q, 0.0), n_valid, axis=0)
            qq_s[h * qr:(h + 1) * qr, :] = jnp.where(row < n_valid, q1, q2)[0:qr]
        m_s[...] = jnp.full(m_s.shape, NEG_BIG, F32)
        l_s[...] = jnp.zeros(l_s.shape, F32)
        acc_s[...] = jnp.zeros(acc_s.shape, F32)

    qq = qq_s[...].astype(BF16)
    col = lax.broadcasted_iota(jnp.int32, (1, prow), 1)
    tcol = (col // H_DIFF).astype(F32)
    own = rowh == (col & (H_DIFF - 1))
    scol = scol_ref[...]
    groups = range(0, pages, DEC_GROUP)
    s_g = []
    for i0 in groups:
        ss = []
        for i in range(i0, i0 + DEC_GROUP):
            s = _dot_nt(qq, k_refs[i][...].astype(BF16))
            base = ((g * pages + i) * PAGE_SIZE - past_len).astype(F32)
            ss.append(jnp.where(own, s + scol * (tcol + base), NEG_BIG))
        s_g.append(jnp.concatenate(ss, axis=1))
    m_g = [jnp.max(s, axis=-1, keepdims=True) for s in s_g]
    p_g = [jnp.exp(s - m) for s, m in zip(s_g, m_g)]
    l_g = [jnp.sum(p, axis=-1, keepdims=True) for p in p_g]
    pv_g = []
    for i0, p32 in zip(groups, p_g):
        p = p32.astype(BF16)
        pv = _dot(p[:, 0:prow], v_refs[i0][...].astype(BF16))
        for i in range(1, DEC_GROUP):
            pv = pv + _dot(p[:, i * prow:(i + 1) * prow], v_refs[i0 + i][...].astype(BF16))
        pv_g.append(pv)
    m_prev = m_s[:, 0:1]
    m_new = m_prev
    for m in m_g:
        m_new = jnp.maximum(m_new, m)
    alpha = jnp.exp(m_prev - m_new)
    l_new = alpha * l_s[:, 0:1]
    acc_new = alpha * acc_s[...]
    for m, l, pv in zip(m_g, l_g, pv_g):
        a = jnp.exp(m - m_new)
        l_new = l_new + a * l
        acc_new = acc_new + a * pv
    m_s[...] = jnp.broadcast_to(m_new, m_s.shape)
    l_s[...] = jnp.broadcast_to(l_new, l_s.shape)
    acc_s[...] = acc_new

    @pl.when(g == pl.num_programs(1) - 1)
    def _():
        coln = lax.broadcasted_iota(jnp.int32, (1, rows), 1)
        tok = lax.broadcasted_iota(jnp.int32, (nr, 1), 0) & (n_valid - 1)
        s = jnp.zeros((nr, rows), F32)
        for h in range(H_DIFF):
            s = jnp.where(rowh == h, _dot_nt(qq, kn_ref[:, h * LANES:(h + 1) * LANES]), s)
        s = s + srow_ref[...] * coln.astype(F32)
        s = jnp.where((coln < n_valid) & (coln <= tok), s, NEG_BIG)
        m_fin = jnp.maximum(m_new, jnp.max(s, axis=-1, keepdims=True))
        alpha = jnp.exp(m_new - m_fin)
        p32 = jnp.exp(s - m_fin)
        l_fin = alpha * l_new + jnp.sum(p32, axis=-1, keepdims=True)
        p = p32.astype(BF16)
        acc = alpha * acc_new
        for h in range(H_DIFF):
            acc = acc + jnp.where(rowh == h, _dot(p, vn_ref[:, h * LANES:(h + 1) * LANES]), 0.0)
        o = acc / l_fin
        d = o - _lam_value(lamp_ref, lam_init) * pltpu.roll(o, nr - n_valid, axis=0)
        d = _rms_rows(d, on_ref[...]) * (1.0 - lam_init)
        row8 = lax.broadcasted_iota(jnp.int32, (qr, 1), 0)
        for h in range(H_DIFF):
            blk = jnp.where(row8 < n_valid, d[h * qr:(h + 1) * qr], 0.0)
            o_ref[:, h * LANES:(h + 1) * LANES] = jnp.concatenate(
                [blk, jnp.zeros((rows - qr, LANES), F32)], axis=0).astype(BF16)


def _diff_decode(q, kn16, vn16, cache_k, cache_v, page_table, slopes, lamp, onorm, layer, bc, rows, n_valid,
                 lam_init, pages):
    n_pages = page_table.shape[1]
    npg = n_pages // pages
    past_len = n_pages * PAGE_SIZE
    width = H_DIFF * LANES
    prow = PAGE_SIZE * H_DIFF
    nr = H_DIFF * 2 * n_valid
    assert pages % DEC_GROUP == 0 and n_valid & (n_valid - 1) == 0 and 2 * n_valid <= rows
    scol = jnp.tile(slopes, PAGE_SIZE).reshape(1, prow)
    srow = jnp.repeat(slopes, 2 * n_valid).reshape(nr, 1)

    def page_spec(i):
        return pl.BlockSpec((None, None, prow, LANES), lambda b, g, pt: (layer, pt[b, g * pages + i], 0, 0))

    tile = pl.BlockSpec((rows, width), lambda b, g, pt: (b, 0))
    grid_spec = pltpu.PrefetchScalarGridSpec(
        num_scalar_prefetch=1,
        grid=(bc, npg),
        in_specs=([pl.BlockSpec((4, DK_DIFF), lambda b, g, pt: (0, 0)),
                   pl.BlockSpec((1, DV_DIFF), lambda b, g, pt: (0, 0)),
                   pl.BlockSpec((1, prow), lambda b, g, pt: (0, 0)),
                   pl.BlockSpec((nr, 1), lambda b, g, pt: (0, 0)),
                   tile, tile, tile]
                  + [page_spec(i) for i in range(pages)] + [page_spec(i) for i in range(pages)]),
        out_specs=tile,
        scratch_shapes=[pltpu.VMEM((nr, LANES), F32), pltpu.VMEM((nr, LANES), F32),
                        pltpu.VMEM((nr, LANES), F32), pltpu.VMEM((nr, LANES), F32)],
    )
    return pl.pallas_call(
        functools.partial(_diff_decode_kernel, pages=pages, rows=rows, n_valid=n_valid, past_len=past_len,
                          lam_init=lam_init),
        grid_spec=grid_spec,
        out_shape=jax.ShapeDtypeStruct((bc * rows, width), BF16),
        compiler_params=_params(("parallel", "arbitrary")),
        name="diff_decode",
    )(page_table, lamp, onorm, scol, srow, q, kn16, vn16, *([cache_k] * pages), *([cache_v] * pages))


def _ca_kernel(q_ref, k_ref, v_ref, o_ref, *, interleaved):
    tq = q_ref.shape[0]
    if interleaved:
        colh = lax.broadcasted_iota(jnp.int32, (1, N_MEM * H_CA), 1) & (H_CA - 1)
        rowh = lax.broadcasted_iota(jnp.int32, (H_CA * tq, 1), 0) // tq
        qs = jnp.concatenate([q_ref[:, h * DH_CA:(h + 1) * DH_CA] for h in range(H_CA)], axis=0)
        s = jnp.where(rowh == colh, _dot_nt(qs, k_ref[...].astype(BF16)), NEG_BIG)
        p = jnp.exp(s - jnp.max(s, axis=-1, keepdims=True))
        a = p / jnp.sum(p, axis=-1, keepdims=True)
        o = _dot(a.astype(BF16), v_ref[...].astype(BF16))
        for h in range(H_CA):
            o_ref[:, h * DH_CA:(h + 1) * DH_CA] = o[h * tq:(h + 1) * tq].astype(BF16)
        return
    for h in range(H_CA):
        hs = slice(h * DH_CA, (h + 1) * DH_CA)
        s = _dot_nt(q_ref[:, hs], k_ref[h])
        p = jnp.exp(s - jnp.max(s, axis=-1, keepdims=True))
        a = p / jnp.sum(p, axis=-1, keepdims=True)
        o_ref[:, hs] = _dot(a.astype(BF16), v_ref[h]).astype(BF16)


def _ca_attn(q, mem_k, mem_v, bc, t, tq, mem_spec, interleaved):
    nq = t // tq
    return pl.pallas_call(
        functools.partial(_ca_kernel, interleaved=interleaved),
        grid=(bc, nq),
        in_specs=[pl.BlockSpec((tq, D_MODEL), lambda b, i: (b * nq + i, 0)), mem_spec, mem_spec],
        out_specs=pl.BlockSpec((tq, D_MODEL), lambda b, i: (b * nq + i, 0)),
        out_shape=jax.ShapeDtypeStruct((bc * t, D_MODEL), BF16),
        compiler_params=_params(("parallel", "arbitrary")),
        name="ca_attn",
    )(q, mem_k, mem_v)


def _block_ones(n, blk):
    i = jnp.arange(n) // blk
    return (i[:, None] == i[None, :])


def _consts(chunk):
    hk, hv = H_GLA * DK_GLA, H_GLA * DV_GLA
    r = jnp.arange(LANES)
    c512 = jnp.arange(4 * LANES)
    c256 = jnp.arange(256)
    i = jnp.arange(chunk)[:, None]
    t = jnp.arange(chunk)[None, :]
    cums = [t <= i]
    if chunk > GLA_SB:
        assert chunk == 4 * GLA_SB
        half = chunk // 2
        cums += [t < GLA_SB * (i // GLA_SB), t <= GLA_SB * (i // GLA_SB) + GLA_SB - 1,
                 t < half * (i // half), t <= half * (i // half) + half - 1]
    return {
        "gla_cum": jnp.concatenate(cums, axis=0).astype(BF16),
        "tri": jnp.tril(jnp.ones((chunk, chunk), F32)).astype(BF16),
        "seg_gla": ((jnp.arange(hk) // DK_GLA)[:, None] == (jnp.arange(hv) // DV_GLA)[None, :]).astype(BF16),
        "bd_gla": ((jnp.arange(hk) // DK_GLA)[:, None] == (jnp.arange(hv) // DV_GLA)[None, :]).astype(F32),
        "p64_256": _block_ones(256, 64).astype(BF16),
        "bd_gdn": _block_ones(256, 64).astype(F32),
        "eg": (r[:, None] == (c512 // LANES)[None, :]).astype(BF16),
        "eg64": (r[:, None] == (c256 // 64)[None, :]).astype(BF16),
        "eb64": (r[:, None] == (c256 // 64 + H_GDN)[None, :]).astype(BF16),
    }


def _expand_state(s, bd):
    b, h, dk, dv = s.shape
    return jnp.tile(s.reshape(b, h * dk, dv), (1, 1, h)) * bd


def _compact_state(s, h):
    b, hk, hv = s.shape
    s5 = s.reshape(b, h, hk // h, h, hv // h)
    return jnp.stack([s5[:, i, :, i, :] for i in range(h)], axis=1)


def _layer(x, wts, layer, depth, kv_stacks, bc, t, chunk, n_valid, tm, s_gla, s_gdn, conv0, ca_fn, diff_fn,
           consts, p64_512):
    lam_init = 0.8 - 0.6 * math.exp(-0.3 * layer)
    gla, q16, k_stack, k16, v_stack, v16, vt16, gdn = _in_proj(
        x, wts["norm_mix"], wts["w_in"], wts["qk_gain"], p64_512, tm, layer, depth, kv_stacks)
    o_gla, s_gla_new = _gla(gla, _expand_state(s_gla, consts["bd_gla"]), wts["gla_w_a2"], wts["gla_b_a"],
                            wts["gla_onorm"], consts, bc, t, chunk, n_valid, min(t, 256),
                            2 if bc % 2 == 0 else 1)
    o_diff = diff_fn(q16, k16, v16, vt16, wts["lamp"], wts["diff_onorm"], lam_init)
    o_gdn, s_gdn_new, conv_new = _gdn(gdn, _expand_state(s_gdn, consts["bd_gdn"]), conv0, wts["gdn_conv_w"],
                                      wts["gdn_a_log"], wts["gdn_dt_bias"], wts["gdn_onorm"], consts, bc, t,
                                      chunk, n_valid, next(n for n in (4, 2, 1) if bc % n == 0))
    x = _mm_res([o_gla, o_diff, o_gdn], wts["w_out"], x, tm)
    qc = _norm_mm(x, wts["norm_ca"], wts["ca_wq"], wts["ca_qnorm"], tm, headnorm=True, scale=DH_CA ** -0.5,
                  out_dtype=BF16)
    oc = ca_fn(qc)
    x = _mm_res([oc], [wts["ca_wo"]], x, tm)
    hid = _swiglu(x, wts["norm_ffn"], wts["ffn_w1"], wts["ffn_w3"], tm)
    x = _mm_res([hid], [wts["ffn_w2"]], x, tm)
    return (x, (k_stack, v_stack), _compact_state(s_gla_new, H_GLA), _compact_state(s_gdn_new, H_GDN), conv_new)


def _pack_w_in(w):
    sizes = (128, 128, 256, 16, 256, 512, 512, 512, 256, 256, 256, 256, 4, 4)
    segs, off = [], 0
    for s in sizes:
        segs.append(w[:, off:off + s])
        off += s
    gq, gk, gv, ga, gg, dq, dk, dv, nq, nk, nv, nz, na, nb = segs
    z = lambda n: jnp.zeros((w.shape[0], n), w.dtype)
    packed = jnp.concatenate([gq, gk, gv, gg, ga, z(GLA_W - 784), dq, dk, dv, nq, nk, nv, nz, na, nb,
                              z(GDN_W - 1032)], axis=1)
    return packed.astype(BF16)


def _row(v, n=None):
    v = v.reshape(1, -1).astype(F32)
    if n is not None and v.shape[1] < n:
        v = jnp.concatenate([v, jnp.zeros((1, n - v.shape[1]), F32)], axis=1)
    return v


def _layer_weights(l, norm_mix, w_in, gla_w_a2, gla_b_a, gla_onorm, diff_qnorm, diff_knorm, lam_q1, lam_k1,
                   lam_q2, lam_k2, diff_onorm, gdn_conv_w, gdn_a_log, gdn_dt_bias, gdn_onorm, w_out, norm_ca,
                   ca_wq, ca_wo, ca_qnorm, norm_ffn, ffn_w1, ffn_w3, ffn_w2):
    wo = w_out[l].astype(BF16)
    a2 = jnp.concatenate([gla_w_a2[l], jnp.zeros((LANES - GLA_RANK, H_GLA * DK_GLA), F32)], axis=0).astype(BF16)
    n_q = 2 * H_DIFF
    return {
        "norm_mix": _row(norm_mix[l]), "w_in": _pack_w_in(w_in[l]),
        "qk_gain": jnp.concatenate([jnp.tile(_row(diff_qnorm[l]), (1, n_q)) * (DK_DIFF ** -0.5),
                                    jnp.tile(_row(diff_knorm[l]), (1, n_q))], axis=1),
        "gla_w_a2": a2, "gla_b_a": _row(gla_b_a[l]), "gla_onorm": jnp.tile(_row(gla_onorm[l]), (1, H_GLA)),
        "lamp": jnp.stack([lam_q1[l], lam_k1[l], lam_q2[l], lam_k2[l]]).astype(F32),
        "diff_onorm": _row(diff_onorm[l]),
        "gdn_conv_w": gdn_conv_w[l].astype(F32), "gdn_a_log": _row(gdn_a_log[l], LANES),
        "gdn_dt_bias": _row(gdn_dt_bias[l], LANES), "gdn_onorm": jnp.tile(_row(gdn_onorm[l]), (1, H_GDN)),
        "w_out": [wo[0:256], wo[256:768], wo[768:1024]],
        "norm_ca": _row(norm_ca[l]), "ca_wq": ca_wq[l].astype(BF16), "ca_wo": ca_wo[l].astype(BF16),
        "ca_qnorm": _row(ca_qnorm[l]), "norm_ffn": _row(norm_ffn[l]),
        "ffn_w1": ffn_w1[l].astype(BF16), "ffn_w3": ffn_w3[l].astype(BF16), "ffn_w2": ffn_w2[l].astype(BF16),
    }


def _forward(x_prompt, x_sample, mem_prompt, cache_k, cache_v, cache_mem_k, cache_mem_v, state_gla, state_gdn,
             state_conv, page_table, norm_mix, w_in, gla_w_a2, gla_b_a, gla_onorm, diff_qnorm, diff_knorm,
             lam_q1, lam_k1, lam_q2, lam_k2, diff_onorm, gdn_conv_w, gdn_a_log, gdn_dt_bias, gdn_onorm, w_out,
             norm_ca, norm_mem, ca_wq, ca_wk, ca_wv, ca_wo, ca_qnorm, ca_knorm, norm_ffn, ffn_w1, ffn_w3, ffn_w2,
             *, chunk_p, tm_p, tq_p, rows_s, pages):
    depth = w_in.shape[0]
    bp, tp, _ = x_prompt.shape
    bs, ts, _ = x_sample.shape
    n_mem = mem_prompt.shape[1]
    slopes = 2.0 ** (-8.0 * jnp.arange(1, H_DIFF + 1, dtype=F32) / H_DIFF)
    consts_p = _consts(chunk_p)
    consts_s = _consts(rows_s)
    p64_512 = _block_ones(DIFF_W, DK_DIFF).astype(BF16)

    xp = x_prompt.reshape(bp * tp, D_MODEL)
    xs = jnp.pad(x_sample, ((0, 0), (0, rows_s - ts), (0, 0))).reshape(bs * rows_s, D_MODEL)
    memp = mem_prompt.reshape(bp * n_mem, D_MODEL)
    zeros_gla = jnp.zeros((bp, H_GLA, DK_GLA, DV_GLA), F32)
    zeros_gdn = jnp.zeros((bp, H_GDN, DK_GDN, DV_GDN), F32)
    zeros_conv = jnp.zeros((bp, CONV_W - 1, C_CONV), F32)
    tm_m = min(bp * n_mem, 512)
    tm_s = bs * rows_s
    cache_k = cache_k.reshape(cache_k.shape[0], cache_k.shape[1], PAGE_SIZE * H_DIFF, LANES)
    cache_v = cache_v.reshape(cache_v.shape[0], cache_v.shape[1], PAGE_SIZE * H_DIFF, LANES)

    mem_k_s = cache_mem_k.reshape(depth, bs, n_mem * H_CA, DH_CA)
    mem_v_s = cache_mem_v.reshape(depth, bs, n_mem * H_CA, DH_CA)
    kv_p = tuple(jnp.zeros((depth, bp * tp * H_DIFF, LANES), F32) for _ in range(2))
    kv_s = tuple(jnp.zeros((depth, bs * rows_s * H_DIFF, LANES), F32) for _ in range(2))

    outs = [[] for _ in range(8)]
    for l in range(depth):
        wts = _layer_weights(l, norm_mix, w_in, gla_w_a2, gla_b_a, gla_onorm, diff_qnorm, diff_knorm, lam_q1,
                             lam_k1, lam_q2, lam_k2, diff_onorm, gdn_conv_w, gdn_a_log, gdn_dt_bias, gdn_onorm,
                             w_out, norm_ca, ca_wq, ca_wo, ca_qnorm, norm_ffn, ffn_w1, ffn_w3, ffn_w2)
        mk, mk16 = _norm_mm(memp, _row(norm_mem[l]), ca_wk[l].astype(BF16), _row(ca_knorm[l]), tm_m, headnorm=True,
                            scale=1.0, out_dtype=F32, head_major_copy=True)
        mv, mv16 = _norm_mm(memp, _row(norm_mem[l]), ca_wv[l].astype(BF16), _row(ca_knorm[l]), tm_m, headnorm=False,
                            scale=1.0, out_dtype=F32, head_major_copy=True)

        def diff_p(q16, k16, v16, vt16, lamp, onorm, lam_init):
            return _diff_prompt(q16, k16, vt16, slopes, lamp, onorm, bp, tp, tq_p, lam_init)

        def ca_p(qc, mk16=mk16, mv16=mv16):
            spec = pl.BlockSpec((H_CA, n_mem, DH_CA), lambda b, i: (0, b, 0))
            return _ca_attn(qc, mk16, mv16, bp, tp, min(tp, 512), spec, False)

        xp, kv_p, sgp, sdp, cvp = _layer(xp, wts, l, depth, kv_p, bp, tp, chunk_p, chunk_p, tm_p, zeros_gla,
                                         zeros_gdn, zeros_conv, ca_p, diff_p, consts_p, p64_512)

        def diff_s(q16, k16, v16, vt16, lamp, onorm, lam_init, l=l):
            return _diff_decode(q16, k16, v16, cache_k, cache_v, page_table, slopes, lamp, onorm, l, bs, rows_s,
                                ts, lam_init, pages)

        def ca_s(qc, l=l):
            spec = pl.BlockSpec((None, None, n_mem * H_CA, DH_CA), lambda b, i: (l, b, 0, 0))
            return _ca_attn(qc, mem_k_s, mem_v_s, bs, rows_s, rows_s, spec, True)

        xs, kv_s, sgs, sds, cvs = _layer(xs, wts, l, depth, kv_s, bs, rows_s, rows_s, ts, tm_s, state_gla[l],
                                         state_gdn[l], state_conv[l], ca_s, diff_s, consts_s, p64_512)
        vals = (sgp, sgs, sdp, sds, cvp, cvs, mk.reshape(bp, n_mem, H_CA, DH_CA), mv.reshape(bp, n_mem, H_CA, DH_CA))
        for o, v in zip(outs, vals):
            o.append(v)

    y_p = xp.reshape(bp, tp, D_MODEL)
    y_s = xs.reshape(bs, rows_s, D_MODEL)[:, :ts]
    kv_out = (kv_p[0].reshape(depth, bp, tp, H_DIFF, 2 * DK_DIFF), kv_p[1].reshape(depth, bp, tp, H_DIFF, DV_DIFF),
              kv_s[0].reshape(depth, bs, rows_s, H_DIFF, 2 * DK_DIFF)[:, :, :ts],
              kv_s[1].reshape(depth, bs, rows_s, H_DIFF, DV_DIFF)[:, :, :ts])
    return (y_p, y_s) + kv_out + tuple(jnp.stack(o) for o in outs)


def kernel(x_prompt, x_sample, mem_prompt, cache_k, cache_v, cache_mem_k, cache_mem_v, state_gla, state_gdn, state_conv, page_table, norm_mix, w_in, gla_w_a2, gla_b_a, gla_onorm, diff_qnorm, diff_knorm, lam_q1, lam_k1, lam_q2, lam_k2, diff_onorm, gdn_conv_w, gdn_a_log, gdn_dt_bias, gdn_onorm, w_out, norm_ca, norm_mem, ca_wq, ca_wk, ca_wv, ca_wo, ca_qnorm, ca_knorm, norm_ffn, ffn_w1, ffn_w3, ffn_w2):
    return _forward(x_prompt, x_sample, mem_prompt, cache_k, cache_v, cache_mem_k, cache_mem_v, state_gla,
                    state_gdn, state_conv, page_table, norm_mix, w_in, gla_w_a2, gla_b_a, gla_onorm, diff_qnorm,
                    diff_knorm, lam_q1, lam_k1, lam_q2, lam_k2, diff_onorm, gdn_conv_w, gdn_a_log, gdn_dt_bias,
                    gdn_onorm, w_out, norm_ca, norm_mem, ca_wq, ca_wk, ca_wv, ca_wo, ca_qnorm, ca_knorm, norm_ffn,
                    ffn_w1, ffn_w3, ffn_w2, chunk_p=64, tm_p=512, tq_p=1024, rows_s=16, pages=16)
```

```python
import functools
import math

import jax
import jax.numpy as jnp
from jax import lax
from jax.experimental import pallas as pl
from jax.experimental.pallas import tpu as pltpu

F32 = jnp.float32
BF16 = jnp.bfloat16

D_MODEL = 1024
DEPTH = 4
PAGE_SIZE = 128
H_GLA, DK_GLA, DV_GLA, GLA_RANK, GLA_TAU = 4, 32, 64, 16, 16.0
H_DIFF, DK_DIFF, DV_DIFF = 4, 64, 128
H_GDN, DK_GDN, DV_GDN, CONV_W = 4, 64, 64, 4
C_CONV = 2 * H_GDN * DK_GDN + H_GDN * DV_GDN
N_MEM, H_CA = 256, 4
DH_CA = D_MODEL // H_CA
D_FF = -(-8 * D_MODEL // (3 * 256)) * 256
RMS_EPS = 1e-6
L2_EPS = 1e-6

VMEM_LIMIT_V7X = 56 * 1024 * 1024
LANES = 128

GLA_W = 896
GDN_W = 1152
DIFF_W = 512
OFF_GLA, OFF_DQ, OFF_DK, OFF_DV, OFF_GDN = 0, 896, 1408, 1920, 2432
IN_W = OFF_GDN + GDN_W

NEG_BIG = -1e30


def _dot(a, b):
    return jnp.dot(a, b, preferred_element_type=F32)


def _dot_nt(a, b):
    return lax.dot_general(a, b, (((1,), (1,)), ((), ())), preferred_element_type=F32)


def _hilo(a):
    hi = a.astype(BF16)
    lo = (a - hi.astype(F32)).astype(BF16)
    return hi, lo


def _dot_sel_rhs(a, sel):
    hi, lo = _hilo(a)
    return _dot(hi, sel) + _dot(lo, sel)


def _dot_sel_lhs(sel, b):
    hi, lo = _hilo(b)
    return _dot(sel, hi) + _dot(sel, lo)


def _dot3(a, b):
    ah, al = _hilo(a)
    bh, bl = _hilo(b)
    return _dot(ah, bh) + _dot(ah, bl) + _dot(al, bh)


def _sigmoid(x):
    return 1.0 / (1.0 + jnp.exp(-x))


def _softplus(x):
    return jnp.maximum(x, 0.0) + jnp.log(1.0 + jnp.exp(-jnp.abs(x)))


def _rms_rows(x, g):
    ms = jnp.mean(x * x, axis=-1, keepdims=True)
    return x * lax.rsqrt(ms + RMS_EPS) * g


def _pad_rows_128(a):
    r = a.shape[0]
    if r == LANES:
        return a
    return jnp.concatenate([a, jnp.zeros((LANES - r, a.shape[1]), a.dtype)], axis=0)


def _const_spec(shape):
    nd = len(shape)
    return pl.BlockSpec(shape, lambda *_: (0,) * nd)


def _params(sem):
    return pltpu.CompilerParams(dimension_semantics=sem, vmem_limit_bytes=VMEM_LIMIT_V7X)


def _in_proj_kernel(x_ref, g_ref, w_ref, qkg_ref, p64_ref, *rest):
    gla_ref, q_ref, k_ref, kb_ref, v_ref, vb_ref, vt_ref, gdn_ref = rest[-8:]
    xn = _rms_rows(x_ref[...], g_ref[...]).astype(BF16)
    gla_ref[...] = _dot(xn, w_ref[:, OFF_GLA:OFF_GLA + GLA_W])

    def qknorm(y, gain):
        ss = _dot_sel_rhs(y * y, p64_ref[...])
        return y * lax.rsqrt(ss * (1.0 / DK_DIFF) + RMS_EPS) * gain

    yq = _dot(xn, w_ref[:, OFF_DQ:OFF_DQ + DIFF_W])
    q_ref[...] = qknorm(yq, qkg_ref[:, 0:DIFF_W]).astype(BF16)
    yk = _dot(xn, w_ref[:, OFF_DK:OFF_DK + DIFF_W])
    kk = qknorm(yk, qkg_ref[:, DIFF_W:2 * DIFF_W])
    kb_ref[...] = kk.astype(BF16)
    yv = _dot(xn, w_ref[:, OFF_DV:OFF_DV + DIFF_W])
    vb_ref[...] = yv.astype(BF16)
    tm = kk.shape[0]
    for h in range(H_DIFF):
        k_ref[pl.ds(h, tm, stride=H_DIFF), :] = kk[:, h * LANES:(h + 1) * LANES]
        v_ref[pl.ds(h, tm, stride=H_DIFF), :] = yv[:, h * LANES:(h + 1) * LANES]
    vt_ref[...] = jnp.transpose(yv).astype(BF16)
    gdn_ref[...] = _dot(xn, w_ref[:, OFF_GDN:OFF_GDN + GDN_W])


def _in_proj(x, g, w, qkg, p64, tm, layer, depth, stacks):
    m = x.shape[0]
    row = lambda wd: pl.BlockSpec((tm, wd), lambda i: (i, 0))
    slot = pl.BlockSpec((None, tm * H_DIFF, LANES), lambda i: (layer, i, 0))
    stack_shape = jax.ShapeDtypeStruct((depth, m * H_DIFF, LANES), F32)
    in_specs = [row(D_MODEL), _const_spec((1, D_MODEL)), _const_spec((D_MODEL, IN_W)),
                _const_spec((1, 2 * DIFF_W)), _const_spec((DIFF_W, DIFF_W))]
    args = [x, g, w, qkg, p64]
    aliases = {}
    if stacks is not None:
        in_specs += [pl.BlockSpec(memory_space=pl.ANY), pl.BlockSpec(memory_space=pl.ANY)]
        args += list(stacks)
        aliases = {5: 2, 6: 4}
    return pl.pallas_call(
        _in_proj_kernel,
        grid=(m // tm,),
        in_specs=in_specs,
        out_specs=[row(GLA_W), row(DIFF_W), slot, row(DIFF_W), slot, row(DIFF_W),
                   pl.BlockSpec((DIFF_W, tm), lambda i: (0, i)), row(GDN_W)],
        out_shape=[jax.ShapeDtypeStruct((m, GLA_W), F32), jax.ShapeDtypeStruct((m, DIFF_W), BF16),
                   stack_shape, jax.ShapeDtypeStruct((m, DIFF_W), BF16),
                   stack_shape, jax.ShapeDtypeStruct((m, DIFF_W), BF16),
                   jax.ShapeDtypeStruct((DIFF_W, m), BF16), jax.ShapeDtypeStruct((m, GDN_W), F32)],
        input_output_aliases=aliases,
        compiler_params=_params(("parallel",)),
        name="in_proj",
    )(*args)


def _norm_mm_kernel(x_ref, g_ref, w_ref, hg_ref, o_ref, *maybe_heads_ref, headnorm, scale):
    xn = _rms_rows(x_ref[...], g_ref[...]).astype(BF16)
    y = _dot(xn, w_ref[...])
    for h in range(H_CA):
        yh = y[:, h * DH_CA:(h + 1) * DH_CA]
        if headnorm:
            yh = _rms_rows(yh, hg_ref[...]) * scale
        o_ref[:, h * DH_CA:(h + 1) * DH_CA] = yh.astype(o_ref.dtype)
        if maybe_heads_ref:
            maybe_heads_ref[0][h] = yh.astype(BF16)


def _norm_mm(x, g, w, hg, tm, *, headnorm, scale, out_dtype, head_major_copy=False):
    m, n = x.shape[0], w.shape[1]
    out_specs = [pl.BlockSpec((tm, n), lambda i: (i, 0))]
    out_shape = [jax.ShapeDtypeStruct((m, n), out_dtype)]
    if head_major_copy:
        out_specs.append(pl.BlockSpec((H_CA, tm, DH_CA), lambda i: (0, i, 0)))
        out_shape.append(jax.ShapeDtypeStruct((H_CA, m, DH_CA), BF16))
    outs = pl.pallas_call(
        functools.partial(_norm_mm_kernel, headnorm=headnorm, scale=scale),
        grid=(m // tm,),
        in_specs=[pl.BlockSpec((tm, D_MODEL), lambda i: (i, 0)), _const_spec((1, D_MODEL)),
                  _const_spec((D_MODEL, n)), _const_spec((1, DH_CA))],
        out_specs=out_specs,
        out_shape=out_shape,
        compiler_params=_params(("parallel",)),
        name="norm_mm",
    )(x, g, w, hg)
    return outs if head_major_copy else outs[0]


def _mm_res_kernel(*refs, n):
    xs, ws, res_ref, o_ref = refs[:n], refs[n:2 * n], refs[2 * n], refs[2 * n + 1]
    acc = res_ref[...]
    for x_ref, w_ref in zip(xs, ws):
        acc = acc + _dot(x_ref[...], w_ref[...])
    o_ref[...] = acc


def _mm_res(xs, ws, res, tm):
    m, n = res.shape
    k = len(xs)
    in_specs = ([pl.BlockSpec((tm, x.shape[1]), lambda i: (i, 0)) for x in xs]
                + [_const_spec(w.shape) for w in ws]
                + [pl.BlockSpec((tm, n), lambda i: (i, 0))])
    return pl.pallas_call(
        functools.partial(_mm_res_kernel, n=k),
        grid=(m // tm,),
        in_specs=in_specs,
        out_specs=pl.BlockSpec((tm, n), lambda i: (i, 0)),
        out_shape=jax.ShapeDtypeStruct((m, n), F32),
        compiler_params=_params(("parallel",)),
        name="mm_res",
    )(*xs, *ws, res)


FF_CHUNK = D_FF // 2


def _swiglu_kernel(x_ref, g_ref, w1_ref, w3_ref, o_ref):
    xn = _rms_rows(x_ref[...], g_ref[...]).astype(BF16)
    for c in range(D_FF // FF_CHUNK):
        sl = slice(c * FF_CHUNK, (c + 1) * FF_CHUNK)
        a = _dot(xn, w1_ref[:, sl])
        b = _dot(xn, w3_ref[:, sl])
        o_ref[:, sl] = (a * _sigmoid(a) * b).astype(BF16)


def _swiglu(x, g, w1, w3, tm):
    m = x.shape[0]
    return pl.pallas_call(
        _swiglu_kernel,
        grid=(m // tm,),
        in_specs=[pl.BlockSpec((tm, D_MODEL), lambda i: (i, 0)), _const_spec((1, D_MODEL)),
                  _const_spec((D_MODEL, D_FF)), _const_spec((D_MODEL, D_FF))],
        out_specs=pl.BlockSpec((tm, D_FF), lambda i: (i, 0)),
        out_shape=jax.ShapeDtypeStruct((m, D_FF), BF16),
        compiler_params=_params(("parallel",)),
        name="swiglu",
    )(x, g, w1, w3)


GLA_SB = 16


def _gla_chunk(p, s, wa_ref, ba_ref, on_ref, cum_ref, seg_ref, p64_ref, bd_ref, b_s, q_s, k_s, v_s, tall_s,
               *, chunk, n_valid):
    hk, hv = H_GLA * DK_GLA, H_GLA * DV_GLA
    nsb = chunk // GLA_SB
    q = p[:, 0:128] * (DK_GLA ** -0.5)
    k = p[:, 128:256]
    v = p[:, 256:512]
    gate = p[:, 512:768]
    x = _dot(p[:, 768:896].astype(BF16), wa_ref[...]) + ba_ref[...]
    yield
    la = (jnp.minimum(x, 0.0) - jnp.log(1.0 + jnp.exp(-jnp.abs(x)))) * (1.0 / GLA_TAU)
    if n_valid < chunk:
        rowv = lax.broadcasted_iota(jnp.int32, (chunk, 1), 0) < n_valid
        la = jnp.where(rowv, la, 0.0)
        k = jnp.where(rowv, k, 0.0)
    cums = _dot_sel_lhs(cum_ref[...], la)
    yield
    b = cums[0:chunk]
    b_s[...] = b
    q_s[...] = q
    k_s[...] = k
    v_s[...] = v
    v16 = v.astype(BF16)
    o = _dot((q * jnp.exp(b)).astype(BF16), s.astype(BF16))
    yield

    if nsb > 1:
        mq, mqc = cums[chunk:2 * chunk], cums[2 * chunk:3 * chunk]
        mh, mhc = cums[3 * chunk:4 * chunk], cums[4 * chunk:5 * chunk]
        lane = lax.broadcasted_iota(jnp.int32, (1, hk), 1)

        def stacked(a):
            return jnp.concatenate(
                [jnp.where((lane >= h * DK_GLA) & (lane < (h + 1) * DK_GLA), a, 0.0) for h in range(H_GLA)],
                axis=0).astype(BF16)

        a2 = _dot_nt(stacked(q * jnp.exp(b - mq)), (k * jnp.exp(mqc - b)).astype(BF16))
        a1 = _dot_nt(stacked(q * jnp.exp(b - mh)), (k * jnp.exp(mhc - b)).astype(BF16))
        yield
        qi =(lax.broadcasted_iota(jnp.int32, (H_GLA * chunk, chunk), 0) & (chunk - 1)) // GLA_SB
        qj = lax.broadcasted_iota(jnp.int32, (H_GLA * chunk, chunk), 1) // GLA_SB
        att = jnp.where(qi == qj + 1, a2, jnp.where(qi > qj + 1, a1, 0.0))
        r = _dot(att.astype(BF16), v16)
        yield
        lane_v = lax.broadcasted_iota(jnp.int32, (1, hv), 1)
        for h in range(H_GLA):
            o = o + jnp.where((lane_v >= h * DV_GLA) & (lane_v < (h + 1) * DV_GLA),
                              r[h * chunk:(h + 1) * chunk], 0.0)

    row = lax.broadcasted_iota(jnp.int32, (GLA_SB, 1), 0)
    for j in range(GLA_SB):
        for d in range(nsb):
            sl = slice(d * GLA_SB, (d + 1) * GLA_SB)
            bj = b_s[d * GLA_SB + j:d * GLA_SB + j + 1, :]
            kj = k_s[d * GLA_SB + j:d * GLA_SB + j + 1, :]
            e = jnp.exp(jnp.where(row >= j, b_s[sl, :] - bj, -jnp.inf))
            tall_s[j * chunk + d * GLA_SB:j * chunk + (d + 1) * GLA_SB, :] = (q_s[sl, :] * kj * e).astype(BF16)
    tsum = _dot(tall_s[...], seg_ref[...])
    yield
    for j in range(GLA_SB):
        vj = jnp.concatenate(
            [jnp.broadcast_to(v_s[d * GLA_SB + j:d * GLA_SB + j + 1, :], (GLA_SB, hv)) for d in range(nsb)], axis=0)
        o = o + tsum[j * chunk:(j + 1) * chunk] * vj

    b_end = b[chunk - 1:chunk, :]
    kd = k * jnp.exp(b_end - b)
    kd_t = jnp.transpose(_pad_rows_128(kd)).astype(BF16)
    upd = _dot(kd_t, _pad_rows_128(v).astype(BF16))
    yield
    dcol = jnp.transpose(jnp.broadcast_to(jnp.exp(b_end), (LANES, LANES)))[:, 0:1]
    s_new = s * dcol + upd * bd_ref[...]

    ss = _dot_sel_rhs(o * o, p64_ref[...])
    on = o * lax.rsqrt(ss * (1.0 / DV_GLA) + RMS_EPS) * on_ref[...]
    return on * gate * _sigmoid(gate), s_new


def _run_stagewise(gens):
    results = [None] * len(gens)
    live = list(range(len(gens)))
    while live:
        for i in list(live):
            try:
                next(gens[i])
            except StopIteration as stop:
                results[i] = stop.value
                live.remove(i)
    return results


def _gla_kernel(p_ref, s0_ref, wa_ref, ba_ref, on_ref, cum_ref, seg_ref, p64_ref, bd_ref,
                o_ref, sout_ref, s_s, b_s, q_s, k_s, v_s, tall_s, *, chunk, n_chunks, n_valid, nseq):
    c = pl.program_id(1)

    @pl.when(c == 0)
    def _():
        s_s[...] = s0_ref[...]

    def body(ci, states):
        r0 = pl.multiple_of(ci * chunk, chunk)
        gens = [_gla_chunk(p_ref[n, pl.ds(r0, chunk), :], states[n], wa_ref, ba_ref, on_ref, cum_ref, seg_ref,
                           p64_ref, bd_ref, b_s.at[n], q_s.at[n], k_s.at[n], v_s.at[n], tall_s.at[n],
                           chunk=chunk, n_valid=n_valid) for n in range(nseq)]
        new_states = []
        for n, (o, s_new) in enumerate(_run_stagewise(gens)):
            o_ref[n, pl.ds(r0, chunk), :] = o.astype(BF16)
            new_states.append(s_new)
        return tuple(new_states)

    states = lax.fori_loop(0, n_chunks, body, tuple(s_s[n] for n in range(nseq)))
    for n in range(nseq):
        s_s[n] = states[n]

    @pl.when(c == pl.num_programs(1) - 1)
    def _():
        for n in range(nseq):
            sout_ref[n] = states[n]


def _gla(proj, s0, wa, ba, onorm, consts, bc, t, chunk, n_valid, block, nseq):
    nb = t // block
    hk, hv = H_GLA * DK_GLA, H_GLA * DV_GLA
    cum = consts["gla_cum"]
    assert bc % nseq == 0
    o, s_out = pl.pallas_call(
        functools.partial(_gla_kernel, chunk=chunk, n_chunks=block // chunk, n_valid=n_valid, nseq=nseq),
        grid=(bc // nseq, nb),
        in_specs=[pl.BlockSpec((nseq, block, GLA_W), lambda b, c: (b, c, 0)),
                  pl.BlockSpec((nseq, hk, hv), lambda b, c: (b, 0, 0)),
                  _const_spec((LANES, hk)), _const_spec((1, hk)), _const_spec((1, hv)),
                  _const_spec(cum.shape), _const_spec((hk, hv)), _const_spec((hv, hv)),
                  _const_spec((hk, hv))],
        out_specs=[pl.BlockSpec((nseq, block, hv), lambda b, c: (b, c, 0)),
                   pl.BlockSpec((nseq, hk, hv), lambda b, c: (b, 0, 0))],
        out_shape=[jax.ShapeDtypeStruct((bc, t, hv), BF16), jax.ShapeDtypeStruct((bc, hk, hv), F32)],
        scratch_shapes=[pltpu.VMEM((nseq, hk, hv), F32), pltpu.VMEM((nseq, chunk, hk), F32),
                        pltpu.VMEM((nseq, chunk, hk), F32), pltpu.VMEM((nseq, chunk, hk), F32),
                        pltpu.VMEM((nseq, chunk, hv), F32), pltpu.VMEM((nseq, GLA_SB * chunk, hk), BF16)],
        compiler_params=_params(("parallel", "arbitrary")),
        name="gla",
    )(proj.reshape(bc, t, GLA_W), s0, wa, ba, onorm, cum, consts["seg_gla"], consts["p64_256"], consts["bd_gla"])
    return o.reshape(bc * t, hv), s_out


def _gdn_kernel(p_ref, s0_ref, cb_ref, cw_ref, alog_ref, dtb_ref, on_ref,
                tri_ref, p64_ref, eg_ref, eg64_ref, eb64_ref, bd_ref,
                o_ref, sout_ref, cout_ref, s_s, xp_s, *, chunk, n_valid, nseq):
    c = pl.program_id(1)
    hd = H_GDN * DK_GDN
    last = c == pl.num_programs(1) - 1

    @pl.when(c == 0)
    def _():
        s_s[...] = s0_ref[...]
        xp_s[:, 5:8, :] = cb_ref[...]

    def l2n(a):
        ss = _dot_sel_rhs(a * a, p64_ref[...])
        return a * lax.rsqrt(ss + L2_EPS)

    ri = lax.broadcasted_iota(jnp.int32, (chunk, chunk), 0)
    ci = lax.broadcasted_iota(jnp.int32, (chunk, chunk), 1)
    eye = (ri == ci).astype(F32)
    lane = lax.broadcasted_iota(jnp.int32, (1, hd), 1)
    lms = [(lane >= h * DK_GDN) & (lane < (h + 1) * DK_GDN) for h in range(H_GDN)]
    n_sq = int(math.log2(chunk)) - 1

    pre = []
    for n in range(nseq):
        p = p_ref[n]
        xp_s[n, 8:8 + chunk, :] = p[:, 0:C_CONV]
        y = xp_s[n, pl.ds(5, chunk), :] * cw_ref[0:1, :]
        for i in range(1, CONV_W):
            y = y + xp_s[n, pl.ds(5 + i, chunk), :] * cw_ref[i:i + 1, :]
        y = y * _sigmoid(y)
        tail = xp_s[n, pl.ds(5 + n_valid, CONV_W - 1), :]
        xp_s[n, 5:8, :] = tail

        @pl.when(last)
        def _(n=n, tail=tail):
            cout_ref[n] = tail

        cq = l2n(y[:, 0:hd]) * (DK_GDN ** -0.5)
        ck = l2n(y[:, hd:2 * hd])
        cv = y[:, 2 * hd:3 * hd]
        small = p[:, 1024:1152]
        g = -jnp.exp(alog_ref[...]) * _softplus(small + dtb_ref[...])
        beta = _sigmoid(small)
        if n_valid < chunk:
            rowv = lax.broadcasted_iota(jnp.int32, (chunk, 1), 0) < n_valid
            g = jnp.where(rowv, g, 0.0)
            beta = jnp.where(rowv, beta, 0.0)
        gam = _dot_sel_lhs(tri_ref[...], g)
        gam_w = _dot_sel_rhs(gam, eg_ref[...])
        gam_b = _dot_sel_rhs(gam, eg64_ref[...])
        beta_b = _dot_sel_rhs(beta, eb64_ref[...])
        gam_t = jnp.transpose(_pad_rows_128(gam))
        eg = jnp.exp(gam_b)
        kb = ck * beta_b
        g_end = gam_b[chunk - 1:chunk, :]
        s = s_s[n]
        pre.append(dict(
            z=p[:, 768:1024], cq=cq, ck16=ck.astype(BF16), eg=eg, kb=kb, kbg=kb * eg, vb=cv * beta_b, g_end=g_end,
            kdec=ck * jnp.exp(g_end - gam_b), s=s, s16=s.astype(BF16),
            decs=[jnp.exp(jnp.where(ri >= ci, gam_w[:, h * LANES:h * LANES + chunk] - gam_t[h:h + 1, 0:chunk],
                                    -jnp.inf)) for h in range(H_GDN)]))

    pairs = [(n, h) for n in range(nseq) for h in range(H_GDN)]
    pws = [-(_dot_nt(jnp.where(lms[h], pre[n]["kb"], 0.0).astype(BF16), pre[n]["ck16"])
             * jnp.where(ri > ci, pre[n]["decs"][h], 0.0)) for n, h in pairs]
    atts = [(_dot_nt(jnp.where(lms[h], pre[n]["cq"], 0.0).astype(BF16), pre[n]["ck16"])
             * pre[n]["decs"][h]).astype(BF16) for n, h in pairs]
    tinvs = [eye + pw for pw in pws]
    for _ in range(n_sq):
        pws = [_dot3(pw, pw) for pw in pws]
        tinvs = [ti + _dot3(ti, pw) for ti, pw in zip(tinvs, pws)]
    sols = [_dot3(ti, jnp.concatenate([jnp.where(lms[h], pre[n]["vb"], 0.0), jnp.where(lms[h], pre[n]["kbg"], 0.0)],
                                      axis=1)) for ti, (n, h) in zip(tinvs, pairs)]

    def seq_sum(n, lo):
        parts = [sols[n * H_GDN + h][:, lo:lo + hd] for h in range(H_GDN)]
        return (parts[0] + parts[1]) + (parts[2] + parts[3])

    v_news = [seq_sum(n, 0) - _dot(seq_sum(n, hd).astype(BF16), pre[n]["s16"]) for n in range(nseq)]
    outs = [_dot((pre[n]["cq"] * pre[n]["eg"]).astype(BF16), pre[n]["s16"]) for n in range(nseq)]
    intra = [_dot(atts[i], jnp.where(lms[h], v_news[n], 0.0).astype(BF16)) for i, (n, h) in enumerate(pairs)]
    upds = [_dot(jnp.transpose(_pad_rows_128(pre[n]["kdec"])).astype(BF16), _pad_rows_128(v_news[n]).astype(BF16))
            for n in range(nseq)]
    for n in range(nseq):
        o = outs[n]
        for h in range(H_GDN):
            o = o + intra[n * H_GDN + h]
        s_new = pre[n]["s"] * jnp.exp(pre[n]["g_end"]) + upds[n] * bd_ref[...]
        s_s[n] = s_new
        ss = _dot_sel_rhs(o * o, p64_ref[...])
        on = o * lax.rsqrt(ss * (1.0 / DV_GDN) + RMS_EPS) * on_ref[...]
        z = pre[n]["z"]
        o_ref[n] = (on * z * _sigmoid(z)).astype(BF16)

        @pl.when(last)
        def _(n=n, s_new=s_new):
            sout_ref[n] = s_new


def _gdn(proj, s0, conv0, cw, alog, dtb, onorm, consts, bc, t, chunk, n_valid, nseq):
    nc = t // chunk
    hd = H_GDN * DK_GDN
    assert bc % nseq == 0
    o, s_out, conv_out = pl.pallas_call(
        functools.partial(_gdn_kernel, chunk=chunk, n_valid=n_valid, nseq=nseq),
        grid=(bc // nseq, nc),
        in_specs=[pl.BlockSpec((nseq, chunk, GDN_W), lambda b, c: (b, c, 0)),
                  pl.BlockSpec((nseq, hd, hd), lambda b, c: (b, 0, 0)),
                  pl.BlockSpec((nseq, CONV_W - 1, C_CONV), lambda b, c: (b, 0, 0)),
                  _const_spec((CONV_W, C_CONV)), _const_spec((1, LANES)), _const_spec((1, LANES)),
                  _const_spec((1, hd)),
                  _const_spec((chunk, chunk)), _const_spec((hd, hd)), _const_spec((LANES, 4 * LANES)),
                  _const_spec((LANES, hd)), _const_spec((LANES, hd)), _const_spec((hd, hd))],
        out_specs=[pl.BlockSpec((nseq, chunk, hd), lambda b, c: (b, c, 0)),
                   pl.BlockSpec((nseq, hd, hd), lambda b, c: (b, 0, 0)),
                   pl.BlockSpec((nseq, CONV_W - 1, C_CONV), lambda b, c: (b, 0, 0))],
        out_shape=[jax.ShapeDtypeStruct((bc, t, hd), BF16), jax.ShapeDtypeStruct((bc, hd, hd), F32),
                   jax.ShapeDtypeStruct((bc, CONV_W - 1, C_CONV), F32)],
        scratch_shapes=[pltpu.VMEM((nseq, hd, hd), F32), pltpu.VMEM((nseq, chunk + 8, C_CONV), F32)],
        compiler_params=_params(("parallel", "arbitrary")),
        name="gdn",
    )(proj.reshape(bc, t, GDN_W), s0, conv0, cw, alog, dtb, onorm, consts["tri"], consts["p64_256"], consts["eg"],
      consts["eg64"], consts["eb64"], consts["bd_gdn"])
    return o.reshape(bc * t, hd), s_out, conv_out


def _lam_value(lamp_ref, lam_init):
    lp = lamp_ref[...]
    return (jnp.exp(jnp.sum(lp[0:1] * lp[1:2], axis=-1, keepdims=True))
            - jnp.exp(jnp.sum(lp[2:3] * lp[3:4], axis=-1, keepdims=True)) + lam_init)


def _softmax_step(carry, s, v16):
    m, l, acc = carry
    m_new = jnp.maximum(m, jnp.max(s, axis=-1, keepdims=True))
    alpha = jnp.exp(m - m_new)
    p = jnp.exp(s - m_new)
    l = alpha * l + jnp.sum(p, axis=-1, keepdims=True)
    acc = alpha * acc + _dot(p.astype(BF16), v16)
    return m_new, l, acc


def _diff_prompt_kernel(slope_ref, lamp_ref, ong_ref, q_ref, k_ref, vt_ref, o_ref, *, tq, lam_init):
    h = pl.program_id(1)
    qi = pl.program_id(2)
    w = 2 * tq
    slope = slope_ref[h]
    lam = _lam_value(lamp_ref, lam_init)
    krow = lax.broadcasted_iota(jnp.int32, (tq, w), 0)
    base = slope * krow.astype(F32)
    rowc = lax.broadcasted_iota(jnp.int32, (LANES, tq), 0)
    q_t = jnp.transpose(q_ref[...].astype(F32))
    qt = jnp.concatenate([jnp.where(rowc < DK_DIFF, q_t, 0.0), jnp.where(rowc >= DK_DIFF, q_t, 0.0)],
                         axis=1).astype(BF16)

    def block(kj, carry, masked):
        m, l, acc = carry
        off = pl.multiple_of(kj * tq, tq)
        s = _dot(k_ref[pl.ds(off, tq), :], qt) + base
        if masked:
            qpos = lax.broadcasted_iota(jnp.int32, (tq, w), 1) & (tq - 1)
            s = jnp.where(krow <= qpos, s, NEG_BIG)
        c = slope * ((kj - qi) * tq).astype(F32)
        m_new = jnp.maximum(m, jnp.max(s, axis=0, keepdims=True) + c)
        p = jnp.exp(s - (m_new - c))
        alpha = jnp.exp(m - m_new)
        l = alpha * l + jnp.sum(p, axis=0, keepdims=True)
        acc = alpha * acc + _dot(vt_ref[:, pl.ds(off, tq)], p.astype(BF16))
        return m_new, l, acc

    carry = (jnp.full((1, w), NEG_BIG, F32), jnp.zeros((1, w), F32), jnp.zeros((LANES, w), F32))
    carry = lax.fori_loop(0, qi, functools.partial(block, masked=False), carry)
    _, l, acc = block(qi, carry, True)
    o = acc / l
    d = o[:, 0:tq] - lam * o[:, tq:w]
    ms = jnp.mean(d * d, axis=0, keepdims=True)
    dn = d * lax.rsqrt(ms + RMS_EPS) * ong_ref[...] * (1.0 - lam_init)
    o_ref[...] = jnp.transpose(dn).astype(BF16)


def _diff_prompt(q, k16, vt16, slopes, lamp, onorm, bc, t, tq, lam_init):
    nq = t // tq
    assert tq & (tq - 1) == 0 and DV_DIFF == LANES
    ong = jnp.broadcast_to(onorm.reshape(DV_DIFF, 1), (DV_DIFF, tq))
    return pl.pallas_call(
        functools.partial(_diff_prompt_kernel, tq=tq, lam_init=lam_init),
        grid=(bc, H_DIFF, nq),
        in_specs=[pl.BlockSpec(memory_space=pltpu.SMEM), _const_spec((4, DK_DIFF)), _const_spec((DV_DIFF, tq)),
                  pl.BlockSpec((tq, LANES), lambda b, h, i: (b * nq + i, h)),
                  pl.BlockSpec((t, LANES), lambda b, h, i: (b, h)),
                  pl.BlockSpec((LANES, t), lambda b, h, i: (h, b))],
        out_specs=pl.BlockSpec((tq, LANES), lambda b, h, i: (b * nq + i, h)),
        out_shape=jax.ShapeDtypeStruct((bc * t, H_DIFF * DV_DIFF), BF16),
        compiler_params=_params(("parallel", "parallel", "arbitrary")),
        name="diff_prompt",
    )(slopes, lamp, ong, q, k16, vt16)


DEC_GROUP = 4


def _diff_decode_kernel(pt_ref, lamp_ref, on_ref, scol_ref, srow_ref, q_ref, kn_ref, vn_ref, *rest,
                        pages, rows, n_valid, past_len, lam_init):
    k_refs, v_refs = rest[:pages], rest[pages:2 * pages]
    o_ref, qq_s, m_s, l_s, acc_s = rest[2 * pages:]
    del pt_ref
    g = pl.program_id(1)
    qr = 2 * n_valid
    nr = H_DIFF * qr
    prow = PAGE_SIZE * H_DIFF
    rowh = lax.broadcasted_iota(jnp.int32, (nr, 1), 0) // qr

    @pl.when(g == 0)
    def _():
        row = lax.broadcasted_iota(jnp.int32, (rows, 1), 0)
        lane = lax.broadcasted_iota(jnp.int32, (rows, LANES), 1)
        for h in range(H_DIFF):
            q = q_ref[:, h * LANES:(h + 1) * LANES].astype(F32)
            q1 = jnp.where(lane < DK_DIFF, q, 0.0)
            q2 = pltpu.roll(jnp.where(lane >= DK_DIFF, q, 0.0), n_valid, axis=0)
            qq_s[h * qr:(h + 1) * qr, :] = jnp.where(row < n_valid, q1, q2)[0:qr]
        m_s[...] = jnp.full(m_s.shape, NEG_BIG, F32)
        l_s[...] = jnp.zeros(l_s.shape, F32)
        acc_s[...] = jnp.zeros(acc_s.shape, F32)

    qq = qq_s[...].astype(BF16)
    col = lax.broadcasted_iota(jnp.int32, (1, prow), 1)
    tcol = (col // H_DIFF).astype(F32)
    own = rowh == (col & (H_DIFF - 1))
    scol = scol_ref[...]
    groups = range(0, pages, DEC_GROUP)
    s_g = []
    for i0 in groups:
        ss = []
        for i in range(i0, i0 + DEC_GROUP):
            s = _dot_nt(qq, k_refs[i][...].astype(BF16))
            base = ((g * pages + i) * PAGE_SIZE - past_len).astype(F32)
            ss.append(jnp.where(own, s + scol * (tcol + base), NEG_BIG))
        s_g.append(jnp.concatenate(ss, axis=1))
    m_g = [jnp.max(s, axis=-1, keepdims=True) for s in s_g]
    p_g = [jnp.exp(s - m) for s, m in zip(s_g, m_g)]
    l_g = [jnp.sum(p, axis=-1, keepdims=True) for p in p_g]
    pv_g = []
    for i0, p32 in zip(groups, p_g):
        p = p32.astype(BF16)
        pv = _dot(p[:, 0:prow], v_refs[i0][...].astype(BF16))
        for i in range(1, DEC_GROUP):
            pv = pv + _dot(p[:, i * prow:(i + 1) * prow], v_refs[i0 + i][...].astype(BF16))
        pv_g.append(pv)
    m_prev = m_s[:, 0:1]
    m_new = m_prev
    for m in m_g:
        m_new = jnp.maximum(m_new, m)
    alpha = jnp.exp(m_prev - m_new)
    l_new = alpha * l_s[:, 0:1]
    acc_new = alpha * acc_s[...]
    for m, l, pv in zip(m_g, l_g, pv_g):
        a = jnp.exp(m - m_new)
        l_new = l_new + a * l
        acc_new = acc_new + a * pv
    m_s[...] = jnp.broadcast_to(m_new, m_s.shape)
    l_s[...] = jnp.broadcast_to(l_new, l_s.shape)
    acc_s[...] = acc_new

    @pl.when(g == pl.num_programs(1) - 1)
    def _():
        coln = lax.broadcasted_iota(jnp.int32, (1, rows), 1)
        tok = lax.broadcasted_iota(jnp.int32, (nr, 1), 0) & (n_valid - 1)
        s = jnp.zeros((nr, rows), F32)
        for h in range(H_DIFF):
            s = jnp.where(rowh == h, _dot_nt(qq, kn_ref[:, h * LANES:(h + 1) * LANES]), s)
        s = s + srow_ref[...] * coln.astype(F32)
        s = jnp.where((coln < n_valid) & (coln <= tok), s, NEG_BIG)
        m_fin = jnp.maximum(m_new, jnp.max(s, axis=-1, keepdims=True))
        alpha = jnp.exp(m_new - m_fin)
        p32 = jnp.exp(s - m_fin)
        l_fin = alpha * l_new + jnp.sum(p32, axis=-1, keepdims=True)
        p = p32.astype(BF16)
        acc = alpha * acc_new
        for h in range(H_DIFF):
            acc = acc + jnp.where(rowh == h, _dot(p, vn_ref[:, h * LANES:(h + 1) * LANES]), 0.0)
        o = acc / l_fin
        d = o - _lam_value(lamp_ref, lam_init) * pltpu.roll(o, nr - n_valid, axis=0)
        d = _rms_rows(d, on_ref[...]) * (1.0 - lam_init)
        row8 = lax.broadcasted_iota(jnp.int32, (qr, 1), 0)
        for h in range(H_DIFF):
            blk = jnp.where(row8 < n_valid, d[h * qr:(h + 1) * qr], 0.0)
            o_ref[:, h * LANES:(h + 1) * LANES] = jnp.concatenate(
                [blk, jnp.zeros((rows - qr, LANES), F32)], axis=0).astype(BF16)


def _diff_decode(q, kn16, vn16, cache_k, cache_v, page_table, slopes, lamp, onorm, layer, bc, rows, n_valid,
                 lam_init, pages):
    n_pages = page_table.shape[1]
    npg = n_pages // pages
    past_len = n_pages * PAGE_SIZE
    width = H_DIFF * LANES
    prow = PAGE_SIZE * H_DIFF
    nr = H_DIFF * 2 * n_valid
    assert pages % DEC_GROUP == 0 and n_valid & (n_valid - 1) == 0 and 2 * n_valid <= rows
    scol = jnp.tile(slopes, PAGE_SIZE).reshape(1, prow)
    srow = jnp.repeat(slopes, 2 * n_valid).reshape(nr, 1)

    def page_spec(i):
        return pl.BlockSpec((None, None, prow, LANES), lambda b, g, pt: (layer, pt[b, g * pages + i], 0, 0))

    tile = pl.BlockSpec((rows, width), lambda b, g, pt: (b, 0))
    grid_spec = pltpu.PrefetchScalarGridSpec(
        num_scalar_prefetch=1,
        grid=(bc, npg),
        in_specs=([pl.BlockSpec((4, DK_DIFF), lambda b, g, pt: (0, 0)),
                   pl.BlockSpec((1, DV_DIFF), lambda b, g, pt: (0, 0)),
                   pl.BlockSpec((1, prow), lambda b, g, pt: (0, 0)),
                   pl.BlockSpec((nr, 1), lambda b, g, pt: (0, 0)),
                   tile, tile, tile]
                  + [page_spec(i) for i in range(pages)] + [page_spec(i) for i in range(pages)]),
        out_specs=tile,
        scratch_shapes=[pltpu.VMEM((nr, LANES), F32), pltpu.VMEM((nr, LANES), F32),
                        pltpu.VMEM((nr, LANES), F32), pltpu.VMEM((nr, LANES), F32)],
    )
    return pl.pallas_call(
        functools.partial(_diff_decode_kernel, pages=pages, rows=rows, n_valid=n_valid, past_len=past_len,
                          lam_init=lam_init),
        grid_spec=grid_spec,
        out_shape=jax.ShapeDtypeStruct((bc * rows, width), BF16),
        compiler_params=_params(("parallel", "arbitrary")),
        name="diff_decode",
    )(page_table, lamp, onorm, scol, srow, q, kn16, vn16, *([cache_k] * pages), *([cache_v] * pages))


def _ca_kernel(q_ref, k_ref, v_ref, o_ref, *, interleaved):
    tq = q_ref.shape[0]
    if interleaved:
        colh = lax.broadcasted_iota(jnp.int32, (1, N_MEM * H_CA), 1) & (H_CA - 1)
        rowh = lax.broadcasted_iota(jnp.int32, (H_CA * tq, 1), 0) // tq
        qs = jnp.concatenate([q_ref[:, h * DH_CA:(h + 1) * DH_CA] for h in range(H_CA)], axis=0)
        s = jnp.where(rowh == colh, _dot_nt(qs, k_ref[...].astype(BF16)), NEG_BIG)
        p = jnp.exp(s - jnp.max(s, axis=-1, keepdims=True))
        a = p / jnp.sum(p, axis=-1, keepdims=True)
        o = _dot(a.astype(BF16), v_ref[...].astype(BF16))
        for h in range(H_CA):
            o_ref[:, h * DH_CA:(h + 1) * DH_CA] = o[h * tq:(h + 1) * tq].astype(BF16)
        return
    for h in range(H_CA):
        hs = slice(h * DH_CA, (h + 1) * DH_CA)
        s = _dot_nt(q_ref[:, hs], k_ref[h])
        p = jnp.exp(s - jnp.max(s, axis=-1, keepdims=True))
        a = p / jnp.sum(p, axis=-1, keepdims=True)
        o_ref[:, hs] = _dot(a.astype(BF16), v_ref[h]).astype(BF16)


def _ca_attn(q, mem_k, mem_v, bc, t, tq, mem_spec, interleaved):
    nq = t // tq
    return pl.pallas_call(
        functools.partial(_ca_kernel, interleaved=interleaved),
        grid=(bc, nq),
        in_specs=[pl.BlockSpec((tq, D_MODEL), lambda b, i: (b * nq + i, 0)), mem_spec, mem_spec],
        out_specs=pl.BlockSpec((tq, D_MODEL), lambda b, i: (b * nq + i, 0)),
        out_shape=jax.ShapeDtypeStruct((bc * t, D_MODEL), BF16),
        compiler_params=_params(("parallel", "arbitrary")),
        name="ca_attn",
    )(q, mem_k, mem_v)


def _block_ones(n, blk):
    i = jnp.arange(n) // blk
    return (i[:, None] == i[None, :])


def _consts(chunk):
    hk, hv = H_GLA * DK_GLA, H_GLA * DV_GLA
    r = jnp.arange(LANES)
    c512 = jnp.arange(4 * LANES)
    c256 = jnp.arange(256)
    i = jnp.arange(chunk)[:, None]
    t = jnp.arange(chunk)[None, :]
    cums = [t <= i]
    if chunk > GLA_SB:
        assert chunk == 4 * GLA_SB
        half = chunk // 2
        cums += [t < GLA_SB * (i // GLA_SB), t <= GLA_SB * (i // GLA_SB) + GLA_SB - 1,
                 t < half * (i // half), t <= half * (i // half) + half - 1]
    return {
        "gla_cum": jnp.concatenate(cums, axis=0).astype(BF16),
        "tri": jnp.tril(jnp.ones((chunk, chunk), F32)).astype(BF16),
        "seg_gla": ((jnp.arange(hk) // DK_GLA)[:, None] == (jnp.arange(hv) // DV_GLA)[None, :]).astype(BF16),
        "bd_gla": ((jnp.arange(hk) // DK_GLA)[:, None] == (jnp.arange(hv) // DV_GLA)[None, :]).astype(F32),
        "p64_256": _block_ones(256, 64).astype(BF16),
        "bd_gdn": _block_ones(256, 64).astype(F32),
        "eg": (r[:, None] == (c512 // LANES)[None, :]).astype(BF16),
        "eg64": (r[:, None] == (c256 // 64)[None, :]).astype(BF16),
        "eb64": (r[:, None] == (c256 // 64 + H_GDN)[None, :]).astype(BF16),
    }


def _expand_state(s, bd):
    b, h, dk, dv = s.shape
    return jnp.tile(s.reshape(b, h * dk, dv), (1, 1, h)) * bd


def _compact_state(s, h):
    b, hk, hv = s.shape
    s5 = s.reshape(b, h, hk // h, h, hv // h)
    return jnp.stack([s5[:, i, :, i, :] for i in range(h)], axis=1)


def _layer(x, wts, layer, depth, kv_stacks, bc, t, chunk, n_valid, tm, s_gla, s_gdn, conv0, ca_fn, diff_fn,
           consts, p64_512):
    lam_init = 0.8 - 0.6 * math.exp(-0.3 * layer)
    gla, q16, k_stack, k16, v_stack, v16, vt16, gdn = _in_proj(
        x, wts["norm_mix"], wts["w_in"], wts["qk_gain"], p64_512, tm, layer, depth, kv_stacks)
    o_gla, s_gla_new = _gla(gla, _expand_state(s_gla, consts["bd_gla"]), wts["gla_w_a2"], wts["gla_b_a"],
                            wts["gla_onorm"], consts, bc, t, chunk, n_valid, min(t, 256),
                            next(n for n in (4, 2, 1) if bc % n == 0))
    o_diff = diff_fn(q16, k16, v16, vt16, wts["lamp"], wts["diff_onorm"], lam_init)
    o_gdn, s_gdn_new, conv_new = _gdn(gdn, _expand_state(s_gdn, consts["bd_gdn"]), conv0, wts["gdn_conv_w"],
                                      wts["gdn_a_log"], wts["gdn_dt_bias"], wts["gdn_onorm"], consts, bc, t,
                                      chunk, n_valid, next(n for n in (4, 2, 1) if bc % n == 0))
    x = _mm_res([o_gla, o_diff, o_gdn], wts["w_out"], x, tm)
    qc = _norm_mm(x, wts["norm_ca"], wts["ca_wq"], wts["ca_qnorm"], tm, headnorm=True, scale=DH_CA ** -0.5,
                  out_dtype=BF16)
    oc = ca_fn(qc)
    x = _mm_res([oc], [wts["ca_wo"]], x, tm)
    hid = _swiglu(x, wts["norm_ffn"], wts["ffn_w1"], wts["ffn_w3"], tm)
    x = _mm_res([hid], [wts["ffn_w2"]], x, tm)
    return (x, (k_stack, v_stack), _compact_state(s_gla_new, H_GLA), _compact_state(s_gdn_new, H_GDN), conv_new)


def _pack_w_in(w):
    sizes = (128, 128, 256, 16, 256, 512, 512, 512, 256, 256, 256, 256, 4, 4)
    segs, off = [], 0
    for s in sizes:
        segs.append(w[:, off:off + s])
        off += s
    gq, gk, gv, ga, gg, dq, dk, dv, nq, nk, nv, nz, na, nb = segs
    z = lambda n: jnp.zeros((w.shape[0], n), w.dtype)
    packed = jnp.concatenate([gq, gk, gv, gg, ga, z(GLA_W - 784), dq, dk, dv, nq, nk, nv, nz, na, nb,
                              z(GDN_W - 1032)], axis=1)
    return packed.astype(BF16)


def _row(v, n=None):
    v = v.reshape(1, -1).astype(F32)
    if n is not None and v.shape[1] < n:
        v = jnp.concatenate([v, jnp.zeros((1, n - v.shape[1]), F32)], axis=1)
    return v


def _layer_weights(l, norm_mix, w_in, gla_w_a2, gla_b_a, gla_onorm, diff_qnorm, diff_knorm, lam_q1, lam_k1,
                   lam_q2, lam_k2, diff_onorm, gdn_conv_w, gdn_a_log, gdn_dt_bias, gdn_onorm, w_out, norm_ca,
                   ca_wq, ca_wo, ca_qnorm, norm_ffn, ffn_w1, ffn_w3, ffn_w2):
    wo = w_out[l].astype(BF16)
    a2 = jnp.concatenate([gla_w_a2[l], jnp.zeros((LANES - GLA_RANK, H_GLA * DK_GLA), F32)], axis=0).astype(BF16)
    n_q = 2 * H_DIFF
    return {
        "norm_mix": _row(norm_mix[l]), "w_in": _pack_w_in(w_in[l]),
        "qk_gain": jnp.concatenate([jnp.tile(_row(diff_qnorm[l]), (1, n_q)) * (DK_DIFF ** -0.5),
                                    jnp.tile(_row(diff_knorm[l]), (1, n_q))], axis=1),
        "gla_w_a2": a2, "gla_b_a": _row(gla_b_a[l]), "gla_onorm": jnp.tile(_row(gla_onorm[l]), (1, H_GLA)),
        "lamp": jnp.stack([lam_q1[l], lam_k1[l], lam_q2[l], lam_k2[l]]).astype(F32),
        "diff_onorm": _row(diff_onorm[l]),
        "gdn_conv_w": gdn_conv_w[l].astype(F32), "gdn_a_log": _row(gdn_a_log[l], LANES),
        "gdn_dt_bias": _row(gdn_dt_bias[l], LANES), "gdn_onorm": jnp.tile(_row(gdn_onorm[l]), (1, H_GDN)),
        "w_out": [wo[0:256], wo[256:768], wo[768:1024]],
        "norm_ca": _row(norm_ca[l]), "ca_wq": ca_wq[l].astype(BF16), "ca_wo": ca_wo[l].astype(BF16),
        "ca_qnorm": _row(ca_qnorm[l]), "norm_ffn": _row(norm_ffn[l]),
        "ffn_w1": ffn_w1[l].astype(BF16), "ffn_w3": ffn_w3[l].astype(BF16), "ffn_w2": ffn_w2[l].astype(BF16),
    }


def _forward(x_prompt, x_sample, mem_prompt, cache_k, cache_v, cache_mem_k, cache_mem_v, state_gla, state_gdn,
             state_conv, page_table, norm_mix, w_in, gla_w_a2, gla_b_a, gla_onorm, diff_qnorm, diff_knorm,
             lam_q1, lam_k1, lam_q2, lam_k2, diff_onorm, gdn_conv_w, gdn_a_log, gdn_dt_bias, gdn_onorm, w_out,
             norm_ca, norm_mem, ca_wq, ca_wk, ca_wv, ca_wo, ca_qnorm, ca_knorm, norm_ffn, ffn_w1, ffn_w3, ffn_w2,
             *, chunk_p, tm_p, tq_p, rows_s, pages):
    depth = w_in.shape[0]
    bp, tp, _ = x_prompt.shape
    bs, ts, _ = x_sample.shape
    n_mem = mem_prompt.shape[1]
    slopes = 2.0 ** (-8.0 * jnp.arange(1, H_DIFF + 1, dtype=F32) / H_DIFF)
    consts_p = _consts(chunk_p)
    consts_s = _consts(rows_s)
    p64_512 = _block_ones(DIFF_W, DK_DIFF).astype(BF16)

    xp = x_prompt.reshape(bp * tp, D_MODEL)
    xs = jnp.pad(x_sample, ((0, 0), (0, rows_s - ts), (0, 0))).reshape(bs * rows_s, D_MODEL)
    memp = mem_prompt.reshape(bp * n_mem, D_MODEL)
    zeros_gla = jnp.zeros((bp, H_GLA, DK_GLA, DV_GLA), F32)
    zeros_gdn = jnp.zeros((bp, H_GDN, DK_GDN, DV_GDN), F32)
    zeros_conv = jnp.zeros((bp, CONV_W - 1, C_CONV), F32)
    tm_m = min(bp * n_mem, 512)
    tm_s = bs * rows_s
    cache_k = cache_k.reshape(cache_k.shape[0], cache_k.shape[1], PAGE_SIZE * H_DIFF, LANES)
    cache_v = cache_v.reshape(cache_v.shape[0], cache_v.shape[1], PAGE_SIZE * H_DIFF, LANES)

    mem_k_s = cache_mem_k.reshape(depth, bs, n_mem * H_CA, DH_CA)
    mem_v_s = cache_mem_v.reshape(depth, bs, n_mem * H_CA, DH_CA)
    kv_p = tuple(jnp.zeros((depth, bp * tp * H_DIFF, LANES), F32) for _ in range(2))
    kv_s = tuple(jnp.zeros((depth, bs * rows_s * H_DIFF, LANES), F32) for _ in range(2))

    outs = [[] for _ in range(8)]
    for l in range(depth):
        wts = _layer_weights(l, norm_mix, w_in, gla_w_a2, gla_b_a, gla_onorm, diff_qnorm, diff_knorm, lam_q1,
                             lam_k1, lam_q2, lam_k2, diff_onorm, gdn_conv_w, gdn_a_log, gdn_dt_bias, gdn_onorm,
                             w_out, norm_ca, ca_wq, ca_wo, ca_qnorm, norm_ffn, ffn_w1, ffn_w3, ffn_w2)
        mk, mk16 = _norm_mm(memp, _row(norm_mem[l]), ca_wk[l].astype(BF16), _row(ca_knorm[l]), tm_m, headnorm=True,
                            scale=1.0, out_dtype=F32, head_major_copy=True)
        mv, mv16 = _norm_mm(memp, _row(norm_mem[l]), ca_wv[l].astype(BF16), _row(ca_knorm[l]), tm_m, headnorm=False,
                            scale=1.0, out_dtype=F32, head_major_copy=True)

        def diff_p(q16, k16, v16, vt16, lamp, onorm, lam_init):
            return _diff_prompt(q16, k16, vt16, slopes, lamp, onorm, bp, tp, tq_p, lam_init)

        def ca_p(qc, mk16=mk16, mv16=mv16):
            spec = pl.BlockSpec((H_CA, n_mem, DH_CA), lambda b, i: (0, b, 0))
            return _ca_attn(qc, mk16, mv16, bp, tp, min(tp, 512), spec, False)

        xp, kv_p, sgp, sdp, cvp = _layer(xp, wts, l, depth, kv_p, bp, tp, chunk_p, chunk_p, tm_p, zeros_gla,
                                         zeros_gdn, zeros_conv, ca_p, diff_p, consts_p, p64_512)

        def diff_s(q16, k16, v16, vt16, lamp, onorm, lam_init, l=l):
            return _diff_decode(q16, k16, v16, cache_k, cache_v, page_table, slopes, lamp, onorm, l, bs, rows_s,
                                ts, lam_init, pages)

        def ca_s(qc, l=l):
            spec = pl.BlockSpec((None, None, n_mem * H_CA, DH_CA), lambda b, i: (l, b, 0, 0))
            return _ca_attn(qc, mem_k_s, mem_v_s, bs, rows_s, rows_s, spec, True)

        xs, kv_s, sgs, sds, cvs = _layer(xs, wts, l, depth, kv_s, bs, rows_s, rows_s, ts, tm_s, state_gla[l],
                                         state_gdn[l], state_conv[l], ca_s, diff_s, consts_s, p64_512)
        vals = (sgp, sgs, sdp, sds, cvp, cvs, mk.reshape(bp, n_mem, H_CA, DH_CA), mv.reshape(bp, n_mem, H_CA, DH_CA))
        for o, v in zip(outs, vals):
            o.append(v)

    y_p = xp.reshape(bp, tp, D_MODEL)
    y_s = xs.reshape(bs, rows_s, D_MODEL)[:, :ts]
    kv_out = (kv_p[0].reshape(depth, bp, tp, H_DIFF, 2 * DK_DIFF), kv_p[1].reshape(depth, bp, tp, H_DIFF, DV_DIFF),
              kv_s[0].reshape(depth, bs, rows_s, H_DIFF, 2 * DK_DIFF)[:, :, :ts],
              kv_s[1].reshape(depth, bs, rows_s, H_DIFF, DV_DIFF)[:, :, :ts])
    return (y_p, y_s) + kv_out + tuple(jnp.stack(o) for o in outs)


def kernel(x_prompt, x_sample, mem_prompt, cache_k, cache_v, cache_mem_k, cache_mem_v, state_gla, state_gdn, state_conv, page_table, norm_mix, w_in, gla_w_a2, gla_b_a, gla_onorm, diff_qnorm, diff_knorm, lam_q1, lam_k1, lam_q2, lam_k2, diff_onorm, gdn_conv_w, gdn_a_log, gdn_dt_bias, gdn_onorm, w_out, norm_ca, norm_mem, ca_wq, ca_wk, ca_wv, ca_wo, ca_qnorm, ca_knorm, norm_ffn, ffn_w1, ffn_w3, ffn_w2):
    return _forward(x_prompt, x_sample, mem_prompt, cache_k, cache_v, cache_mem_k, cache_mem_v, state_gla,
                    state_gdn, state_conv, page_table, norm_mix, w_in, gla_w_a2, gla_b_a, gla_onorm, diff_qnorm,
                    diff_knorm, lam_q1, lam_k1, lam_q2, lam_k2, diff_onorm, gdn_conv_w, gdn_a_log, gdn_dt_bias,
                    gdn_onorm, w_out, norm_ca, norm_mem, ca_wq, ca_wk, ca_wv, ca_wo, ca_qnorm, ca_knorm, norm_ffn,
                    ffn_w1, ffn_w3, ffn_w2, chunk_p=64, tm_p=512, tq_p=1024, rows_s=16, pages=16)
```
